```python
import math
import jax, jax.numpy as jnp
from jax import lax
import numpy as np

D_MODEL = 1024
BATCH = 4
SEQ = 4096
DEPTH = 2
DEC_BATCH = 32
DEC_SEQ = 8
PAST_LEN = 8192
PAGE_SIZE = 128

N_META = 16
GLA_HEADS = 4
GLA_DK = 128
GLA_DV = 256
GLA_RANK = 16
GLA_TAU = 16.0
GLA_CHUNK = 64
DSA_HEADS = 8
DSA_KV_HEADS = 2
DSA_HD = 128
IDX_HEADS = 8
IDX_DIM = 64
TOPK_MAX = 256
Q_BLOCK = 128
REL_BUCKETS = 32
REL_MAX_DIST = 128
D_FF = 4096
CONV_W = 3
EPS = 1e-6

GLA_KW = GLA_HEADS * GLA_DK
GLA_VW = GLA_HEADS * GLA_DV
DSA_QW = DSA_HEADS * DSA_HD
DSA_KVW = DSA_KV_HEADS * DSA_HD
SPLITS = (GLA_KW, GLA_KW, GLA_VW, GLA_VW, GLA_RANK,
          DSA_QW, DSA_KVW, DSA_KVW, IDX_HEADS * IDX_DIM, IDX_HEADS, IDX_DIM,
          D_MODEL, D_MODEL)
P_TOTAL = sum(SPLITS)

kernel_name = 'gla_dsa_parallel_hybrid_step'


def split_cols(z):
    offs = np.cumsum(SPLITS)[:-1].tolist()
    return jnp.split(z, offs, axis=-1)


def rms_norm(x, g):
    xf = x.astype(jnp.float32)
    y = xf * lax.rsqrt(jnp.mean(xf * xf, axis=-1, keepdims=True) + EPS)
    return (y * g.astype(jnp.float32)).astype(x.dtype)


def rel_bucket(dist):
    max_exact = REL_BUCKETS // 2
    d = jnp.maximum(dist, 0)
    large = max_exact + (jnp.log(jnp.maximum(d, 1).astype(jnp.float32) / max_exact)
                         / math.log(REL_MAX_DIST / max_exact) * (REL_BUCKETS - max_exact)).astype(jnp.int32)
    large = jnp.minimum(large, REL_BUCKETS - 1)
    return jnp.where(d < max_exact, d, large)


def project_mixer(h, w_in, w_a2, b_a):
    B, T = h.shape[:2]
    (gq, gk, gv, gr, ga1, dq, dk, dv, iq, iw, ik, gate_a, gate_b) = split_cols(h @ w_in)
    log_a = jax.nn.log_sigmoid((ga1 @ w_a2 + b_a).astype(jnp.float32)) / GLA_TAU
    gla = (gq.reshape(B, T, GLA_HEADS, GLA_DK) * GLA_DK ** -0.5,
           gk.reshape(B, T, GLA_HEADS, GLA_DK),
           gv.reshape(B, T, GLA_HEADS, GLA_DV),
           log_a.reshape(B, T, GLA_HEADS, GLA_DK),
           gr)
    dsa = (dq.reshape(B, T, DSA_HEADS, DSA_HD),
           dk.reshape(B, T, DSA_KV_HEADS, DSA_HD),
           dv.reshape(B, T, DSA_KV_HEADS, DSA_HD),
           iq.reshape(B, T, IDX_HEADS, IDX_DIM),
           iw, ik)
    return gla, dsa, gate_a, gate_b


def gla_recurrence(q, k, v, log_a, s0, front_pad, chunk):
    B, T, H = q.shape[:3]
    tail = (-(front_pad + T)) % chunk

    def prep(z):
        z = jnp.pad(z.astype(jnp.float32), ((0, 0), (front_pad, tail), (0, 0), (0, 0)))
        n = z.shape[1] // chunk
        return z.reshape(B, n, chunk, H, z.shape[3]).transpose(1, 0, 3, 2, 4)

    causal = jnp.tril(jnp.ones((chunk, chunk), bool))[:, :, None]

    def step(S, inp):
        qc, kc, vc, lac = inp
        b = jnp.cumsum(lac, axis=2)
        inter = jnp.einsum('bhtd,bhde->bhte', qc * jnp.exp(b), S)
        diff = b[:, :, :, None, :] - b[:, :, None, :, :]
        decay = jnp.where(causal, jnp.exp(jnp.where(causal, diff, 0.0)), 0.0)
        att = jnp.einsum('bhtd,bhsd,bhtsd->bhts', qc, kc, decay)
        intra = jnp.einsum('bhts,bhse->bhte', att, vc)
        b_last = b[:, :, -1, :]
        S_new = S * jnp.exp(b_last)[..., None] + jnp.einsum(
            'bhsd,bhse->bhde', kc * jnp.exp(b_last[:, :, None, :] - b), vc)
        return S_new, inter + intra

    S_fin, out = lax.scan(step, s0.astype(jnp.float32), (prep(q), prep(k), prep(v), prep(log_a)))
    out = out.transpose(1, 0, 3, 2, 4).reshape(B, -1, H, GLA_DV)
    return out[:, front_pad:front_pad + T], S_fin


def gla_output(o, gr, gain):
    B, T = o.shape[:2]
    o = o * lax.rsqrt(jnp.mean(o * o, axis=-1, keepdims=True) + EPS) * gain.astype(jnp.float32)
    return o.reshape(B, T, GLA_VW).astype(gr.dtype) * jax.nn.silu(gr)


def indexer_scores(qi, wi, ki):
    s = jax.nn.relu(jnp.einsum('bqhd,bsd->bqhs', qi, ki).astype(jnp.float32) * IDX_DIM ** -0.5)
    return jnp.einsum('bqh,bqhs->bqs', wi.astype(jnp.float32) * IDX_HEADS ** -0.5, s)


def sparse_attend(qb, kg, vg, dist, valid, rel_table):
    B, Q = qb.shape[:2]
    G = DSA_HEADS // DSA_KV_HEADS
    qg = qb.reshape(B, Q, DSA_KV_HEADS, G, DSA_HD)
    logits = jnp.einsum('bqgrd,bqkgd->bqgrk', qg, kg).astype(jnp.float32) * DSA_HD ** -0.5
    bias = rel_table[rel_bucket(dist)].astype(jnp.float32)
    bias = bias.reshape(B, Q, -1, DSA_KV_HEADS, G).transpose(0, 1, 3, 4, 2)
    logits = jnp.where(valid[:, :, None, None, :], logits + bias, -jnp.inf)
    p = jax.nn.softmax(logits, axis=-1).astype(vg.dtype)
    o = jnp.einsum('bqgrk,bqkgd->bqgrd', p, vg)
    return o.reshape(B, Q, DSA_QW)


def gather_rows(z, idx):
    return jax.vmap(lambda zb, ib: zb[ib])(z, idx)


def dsa_prompt(q, k, v, qi, wi, ki, rel_table):
    B, T = q.shape[:2]
    k_top = min(TOPK_MAX, T // 4)
    n_blk = -(-T // Q_BLOCK)
    pad = n_blk * Q_BLOCK - T

    def blocks(z):
        z = jnp.pad(z, ((0, 0), (0, pad)) + ((0, 0),) * (z.ndim - 2))
        return z.reshape((B, n_blk, Q_BLOCK) + z.shape[2:]).swapaxes(0, 1)

    qpos = jnp.arange(n_blk * Q_BLOCK, dtype=jnp.int32).reshape(n_blk, Q_BLOCK)
    key_pos = jnp.arange(T, dtype=jnp.int32)

    def one_block(inp):
        qb, qib, wib, pb = inp
        sc = indexer_scores(qib, wib, ki)
        sc = jnp.where(key_pos[None, None, :] <= pb[None, :, None], sc, -jnp.inf)
        _, sel = lax.top_k(sc, k_top)
        kg = gather_rows(k, sel)
        vg = gather_rows(v, sel)
        dist = pb[None, :, None] - sel
        return sparse_attend(qb, kg, vg, dist, dist >= 0, rel_table)

    out = lax.map(one_block, (blocks(q), blocks(qi), blocks(wi), qpos))
    return out.swapaxes(0, 1).reshape(B, n_blk * Q_BLOCK, DSA_QW)[:, :T]


def dsa_sample(q, k_new, v_new, qi, wi, ki_new, ck, cv, cki, page_table, rel_table):
    DB, NQ = q.shape[:2]
    L = PAST_LEN + NQ
    k_top = min(TOPK_MAX, L // 4)
    ki_past = cki[page_table].reshape(DB, PAST_LEN, IDX_DIM)
    ki_all = jnp.concatenate([ki_past, ki_new.astype(ki_past.dtype)], axis=1)
    qpos = PAST_LEN + jnp.arange(NQ, dtype=jnp.int32)
    key_pos = jnp.arange(L, dtype=jnp.int32)
    sc = indexer_scores(qi, wi, ki_all)
    sc = jnp.where(key_pos[None, None, :] <= qpos[None, :, None], sc, -jnp.inf)
    _, sel = lax.top_k(sc, k_top)
    in_past = sel < PAST_LEN
    sp = jnp.minimum(sel, PAST_LEN - 1)
    phys = jax.vmap(lambda pt, s: pt[s])(page_table, sp // PAGE_SIZE)
    off = sp % PAGE_SIZE
    sn = jnp.clip(sel - PAST_LEN, 0, NQ - 1)

    def gather(cache, new):
        past = cache[phys, off]
        cur = gather_rows(new, sn).astype(past.dtype)
        return jnp.where(in_past[..., None, None], past, cur)

    kg = gather(ck, k_new)
    vg = gather(cv, v_new)
    dist = qpos[None, :, None] - sel
    return sparse_attend(q, kg, vg, dist, dist >= 0, rel_table)


def merge_branches(o_a, o_b, gate_a, gate_b, w_pa, w_pb, w_out):
    return (jax.nn.sigmoid(gate_a) * (o_a @ w_pa) + jax.nn.sigmoid(gate_b) * (o_b @ w_pb)) @ w_out


def conv_ffn(h, prev, w_ffn_in, conv_w, conv_b, w_ffn_down):
    T = h.shape[1]
    g, u = jnp.split(h @ w_ffn_in, 2, axis=-1)
    gp = jnp.concatenate([prev.astype(g.dtype), g], axis=1)
    c = conv_b + gp[:, 0:T] * conv_w[0]
    for j in range(1, CONV_W):
        c = c + gp[:, j:j + T] * conv_w[j]
    y = (jax.nn.gelu(c, approximate=True) * u) @ w_ffn_down
    return y, gp[:, -(CONV_W - 1):]


def setup_inputs(seed: int = 0) -> dict:
    key = jax.random.key(seed)
    ks = jax.random.split(key, 32)
    f32 = jnp.float32
    n_pages = PAST_LEN // PAGE_SIZE
    n_pool = (DEC_BATCH * n_pages * 5) // 4

    def nrm(k, shape, s):
        return jax.random.normal(k, shape, f32) * s

    page_table = jax.random.permutation(ks[7], n_pool)[:DEC_BATCH * n_pages].reshape(
        DEC_BATCH, n_pages).astype(jnp.int32)
    return {
        'x_prompt': nrm(ks[0], (BATCH, SEQ, D_MODEL), 1.0),
        'x_sample': nrm(ks[1], (DEC_BATCH, DEC_SEQ, D_MODEL), 1.0),
        'cache_k': nrm(ks[2], (DEPTH, n_pool, PAGE_SIZE, DSA_KV_HEADS, DSA_HD), 1.0),
        'cache_v': nrm(ks[3], (DEPTH, n_pool, PAGE_SIZE, DSA_KV_HEADS, DSA_HD), 1.0),
        'cache_idx_k': nrm(ks[4], (DEPTH, n_pool, PAGE_SIZE, IDX_DIM), 1.0),
        'state_gla': nrm(ks[5], (DEPTH, DEC_BATCH, GLA_HEADS, GLA_DK, GLA_DV), 0.5),
        'state_conv': nrm(ks[6], (DEPTH, DEC_BATCH, CONV_W - 1, D_FF), 1.0),
        'page_table': page_table,
        'meta_tokens': nrm(ks[8], (N_META, D_MODEL), 1.0),
        'rel_bias': nrm(ks[9], (REL_BUCKETS, DSA_HEADS), 0.5),
        'norm_mix_pre': 1.0 + nrm(ks[10], (DEPTH, D_MODEL), 0.01),
        'norm_mix_post': 1.0 + nrm(ks[11], (DEPTH, D_MODEL), 0.01),
        'norm_ffn_pre': 1.0 + nrm(ks[12], (DEPTH, D_MODEL), 0.01),
        'norm_ffn_post': 1.0 + nrm(ks[13], (DEPTH, D_MODEL), 0.01),
        'w_in': nrm(ks[14], (DEPTH, D_MODEL, P_TOTAL), D_MODEL ** -0.5),
        'w_a2': nrm(ks[15], (DEPTH, GLA_RANK, GLA_KW), GLA_RANK ** -0.5),
        'b_a': nrm(ks[16], (DEPTH, GLA_KW), 0.1),
        'gla_norm': 1.0 + nrm(ks[17], (DEPTH, GLA_DV), 0.01),
        'w_pa': nrm(ks[18], (DEPTH, GLA_VW, D_MODEL), GLA_VW ** -0.5),
        'w_pb': nrm(ks[19], (DEPTH, DSA_QW, D_MODEL), DSA_QW ** -0.5),
        'w_out': nrm(ks[20], (DEPTH, D_MODEL, D_MODEL), D_MODEL ** -0.5),
        'w_ffn_in': nrm(ks[21], (DEPTH, D_MODEL, 2 * D_FF), D_MODEL ** -0.5),
        'conv_w': nrm(ks[22], (DEPTH, CONV_W, D_FF), CONV_W ** -0.5),
        'conv_b': nrm(ks[23], (DEPTH, D_FF), 0.01),
        'w_ffn_down': nrm(ks[24], (DEPTH, D_FF, D_MODEL), D_FF ** -0.5),
    }


def reference(x_prompt, x_sample, cache_k, cache_v, cache_idx_k, state_gla, state_conv, page_table,
              meta_tokens, rel_bias, norm_mix_pre, norm_mix_post, norm_ffn_pre, norm_ffn_post,
              w_in, w_a2, b_a, gla_norm, w_pa, w_pb, w_out, w_ffn_in, conv_w, conv_b, w_ffn_down):
    B = x_prompt.shape[0]
    xp = jnp.concatenate([jnp.broadcast_to(meta_tokens[None].astype(x_prompt.dtype), (B, N_META, D_MODEL)),
                          x_prompt], axis=1)
    pk, pv, pik, pgla, pconv = [], [], [], [], []
    for l in range(DEPTH):
        h = rms_norm(xp, norm_mix_pre[l])
        gla, dsa, ga, gb = project_mixer(h, w_in[l], w_a2[l], b_a[l])
        gq, gk, gv, gla_la, gr = gla
        s0 = jnp.zeros((B, GLA_HEADS, GLA_DK, GLA_DV), jnp.float32)
        o, s_fin = gla_recurrence(gq, gk, gv, gla_la, s0, GLA_CHUNK - N_META, GLA_CHUNK)
        o_a = gla_output(o, gr, gla_norm[l])
        dq, dk, dv, iq, iw, ik = dsa
        o_b = dsa_prompt(dq, dk, dv, iq, iw, ik, rel_bias)
        m = merge_branches(o_a, o_b, ga, gb, w_pa[l], w_pb[l], w_out[l])
        xp = xp + rms_norm(m, norm_mix_post[l])
        h2 = rms_norm(xp, norm_ffn_pre[l])
        f, c_state = conv_ffn(h2, jnp.zeros((B, CONV_W - 1, D_FF), h2.dtype),
                              w_ffn_in[l], conv_w[l], conv_b[l], w_ffn_down[l])
        xp = xp + rms_norm(f, norm_ffn_post[l])
        pk.append(dk); pv.append(dv); pik.append(ik); pgla.append(s_fin); pconv.append(c_state)
    y_prompt = xp[:, N_META:]

    xs = x_sample
    sk, sv, sik, sgla, sconv = [], [], [], [], []
    chunk_s = min(GLA_CHUNK, xs.shape[1])
    for l in range(DEPTH):
        h = rms_norm(xs, norm_mix_pre[l])
        gla, dsa, ga, gb = project_mixer(h, w_in[l], w_a2[l], b_a[l])
        gq, gk, gv, gla_la, gr = gla
        o, s_fin = gla_recurrence(gq, gk, gv, gla_la, state_gla[l], 0, chunk_s)
        o_a = gla_output(o, gr, gla_norm[l])
        dq, dk, dv, iq, iw, ik = dsa
        o_b = dsa_sample(dq, dk, dv, iq, iw, ik, cache_k[l], cache_v[l], cache_idx_k[l], page_table, rel_bias)
        m = merge_branches(o_a, o_b, ga, gb, w_pa[l], w_pb[l], w_out[l])
        xs = xs + rms_norm(m, norm_mix_post[l])
        h2 = rms_norm(xs, norm_ffn_pre[l])
        f, c_state = conv_ffn(h2, state_conv[l], w_ffn_in[l], conv_w[l], conv_b[l], w_ffn_down[l])
        xs = xs + rms_norm(f, norm_ffn_post[l])
        sk.append(dk); sv.append(dv); sik.append(ik); sgla.append(s_fin); sconv.append(c_state)
    y_sample = xs

    return (y_prompt, y_sample,
            jnp.stack(pk), jnp.stack(pv), jnp.stack(pik), jnp.stack(pgla), jnp.stack(pconv),
            jnp.stack(sk), jnp.stack(sv), jnp.stack(sik), jnp.stack(sgla), jnp.stack(sconv))
```

```python
import functools
import math

import jax
import jax.numpy as jnp
import numpy as np
from jax import lax
from jax.experimental import pallas as pl
from jax.experimental.pallas import tpu as pltpu

F32 = jnp.float32
BF16 = jnp.bfloat16
I32 = jnp.int32

D_MODEL = 1024
N_META = 16
GLA_HEADS = 4
GLA_DK = 128
GLA_DV = 256
GLA_RANK = 16
GLA_TAU = 16.0
DSA_HEADS = 8
DSA_KV_HEADS = 2
DSA_HD = 128
IDX_HEADS = 8
IDX_DIM = 64
TOPK_MAX = 256
REL_BUCKETS = 32
REL_MAX_DIST = 128
D_FF = 4096
CONV_W = 3
EPS = 1e-6
PAGE_SIZE = 128

GLA_KW = GLA_HEADS * GLA_DK
GLA_VW = GLA_HEADS * GLA_DV
DSA_QW = DSA_HEADS * DSA_HD
DSA_KVW = DSA_KV_HEADS * DSA_HD
IDX_QW = IDX_HEADS * IDX_DIM

LANES = 128
SUBLANES = 8
QB = 128
VMEM_LIMIT = 56 * 1024 * 1024

INT_MIN = -(2 ** 31)
NEG_BIG = -1e30

SEGS = (("gq", GLA_KW), ("gk", GLA_KW), ("gv", GLA_VW), ("gr", GLA_VW), ("dq", DSA_QW),
        ("dk", DSA_KVW), ("dv", DSA_KVW), ("iq", IDX_QW), ("ga", D_MODEL), ("gb", D_MODEL),
        ("sm", LANES))
SM_IK = 0
SM_GA1 = IDX_DIM
SM_IW = IDX_DIM + GLA_RANK
SEG_OFFS = tuple(int(v) for v in np.cumsum([0] + [w for _, w in SEGS]))
PW = SEG_OFFS[-1]


def _relayout_w_in(w):
    o = np.cumsum([0, GLA_KW, GLA_KW, GLA_VW, GLA_VW, GLA_RANK, DSA_QW, DSA_KVW, DSA_KVW,
                   IDX_QW, IDX_HEADS, IDX_DIM, D_MODEL, D_MODEL]).tolist()
    c = lambda i: w[:, o[i]:o[i + 1]]
    pad = jnp.zeros((w.shape[0], LANES - IDX_DIM - GLA_RANK - IDX_HEADS), w.dtype)
    cols = [c(0), c(1), c(2), c(3), c(5), c(6), c(7), c(8), c(11), c(12), c(10), c(4), c(9), pad]
    return jnp.concatenate(cols, axis=1).astype(BF16)


def _rms(x, g):
    return x * lax.rsqrt(jnp.mean(x * x, axis=-1, keepdims=True) + EPS) * g


def _sigmoid(x):
    return 1.0 / (1.0 + jnp.exp(-x))


def _dot(a, b):
    return jnp.dot(a, b, preferred_element_type=F32)


def _dot_nt(a, b):
    return lax.dot_general(a, b, (((1,), (1,)), ((), ())), preferred_element_type=F32)


def _params(sem):
    return pltpu.CompilerParams(dimension_semantics=sem, vmem_limit_bytes=VMEM_LIMIT)


def _proj_kernel(x_ref, g_ref, w_ref, *out_refs):
    hb = _rms(x_ref[...], g_ref[...]).astype(BF16)
    for o_ref, off in zip(out_refs, SEG_OFFS[:-1]):
        wd = o_ref.shape[-1]
        o_ref[...] = _dot(hb, w_ref[:, off:off + wd])


def _proj(x, g, w, tm):
    n = x.shape[0]
    assert n % tm == 0
    return pl.pallas_call(
        _proj_kernel,
        grid=(n // tm,),
        in_specs=[pl.BlockSpec((tm, D_MODEL), lambda i: (i, 0)),
                  pl.BlockSpec((1, D_MODEL), lambda i: (0, 0)),
                  pl.BlockSpec((D_MODEL, PW), lambda i: (0, 0))],
        out_specs=[pl.BlockSpec((tm, wd), lambda i: (i, 0)) for _, wd in SEGS],
        out_shape=[jax.ShapeDtypeStruct((n, wd), F32) for _, wd in SEGS],
        compiler_params=_params(("arbitrary",)),
        name="proj",
    )(x, g, w)


def _gla_kernel(q_ref, k_ref, v_ref, gr_ref, sm_ref, wa2_ref, ba_ref, gain_ref, s0_ref,
                o_ref, sfin_ref, s_ref, *, t_valid, rows):
    C = QB
    c = pl.program_id(2)

    @pl.when(c == 0)
    def _():
        s_ref[...] = s0_ref[0, 0]

    def padrows(z):
        if rows == C:
            return z
        return jnp.concatenate([z, jnp.zeros((C - rows, z.shape[1]), z.dtype)], axis=0)

    rowi = lax.broadcasted_iota(I32, (C, 1), 0)
    live = c * C + rowi < t_valid
    q = padrows(q_ref[0]) * (GLA_DK ** -0.5)
    k = jnp.where(live, padrows(k_ref[0]), 0.0)
    v = padrows(v_ref[0])
    ga1 = padrows(sm_ref[0])[:, SM_GA1:SM_GA1 + GLA_RANK]
    x = jnp.dot(ga1, wa2_ref[...], precision=lax.Precision.HIGHEST,
                preferred_element_type=F32) + ba_ref[...]
    la = (jnp.minimum(x, 0.0) - jnp.log1p(jnp.exp(-jnp.abs(x)))) * (1.0 / GLA_TAU)
    la = jnp.where(live, la, 0.0)
    ti = lax.broadcasted_iota(I32, (C, C), 0)
    si = lax.broadcasted_iota(I32, (C, C), 1)
    tri = (ti >= si).astype(F32)
    b = jnp.dot(tri, la, precision=lax.Precision.HIGHEST, preferred_element_type=F32)

    att = None
    m = C // 2
    while m >= SUBLANES:
        pieces = []
        for p in range(C // (2 * m)):
            r = p * 2 * m + m - 1
            pieces.append(jnp.broadcast_to(b[r:r + 1, :], (2 * m, GLA_DK)))
        bound = pieces[0] if len(pieces) == 1 else jnp.concatenate(pieces, axis=0)
        upper = ((rowi // m) % 2) == 1
        d = b - bound
        qm = jnp.where(upper, q * jnp.exp(jnp.minimum(d, 0.0)), 0.0)
        km = jnp.where(upper, 0.0, k * jnp.exp(jnp.minimum(-d, 0.0)))
        a = _dot_nt(qm.astype(BF16), km.astype(BF16))
        if 2 * m < C:
            a = jnp.where((ti // (2 * m)) == (si // (2 * m)), a, 0.0)
        att = a if att is None else att + a
        m //= 2

    tm8 = rowi % SUBLANES
    intra_d = jnp.zeros((C, GLA_DV), F32)
    for delta in range(SUBLANES):
        if delta == 0:
            kd, bd, vd = k, b, v
        else:
            kd = pltpu.roll(k, delta, 0)
            bd = pltpu.roll(b, delta, 0)
            vd = pltpu.roll(v, delta, 0)
        w = q * kd * jnp.exp(jnp.minimum(b - bd, 0.0))
        a = jnp.sum(w, axis=1, keepdims=True)
        a = jnp.where(tm8 >= delta, a, 0.0)
        intra_d = intra_d + a * vd

    s = s_ref[...]
    vb = v.astype(BF16)
    inter = _dot((q * jnp.exp(b)).astype(BF16), s.astype(BF16))
    o = inter + _dot(att.astype(BF16), vb) + intra_d

    b_last_row = b[C - 1:C, :]
    b_last_col = b.T[:, C - 1:C]
    kdec = k * jnp.exp(b_last_row - b)
    s_new = s * jnp.exp(b_last_col) + _dot(kdec.T.astype(BF16), vb)
    s_ref[...] = s_new

    on = o * lax.rsqrt(jnp.mean(o * o, axis=-1, keepdims=True) + EPS) * gain_ref[...]
    gr = padrows(gr_ref[0])
    res = on * (gr * _sigmoid(gr))
    o_ref[0] = res[:rows].astype(o_ref.dtype)

    @pl.when(c == pl.num_programs(2) - 1)
    def _():
        sfin_ref[0, 0] = s_new


def _gla(gq, gk, gv, gr, sm, wa2, ba, gain, s0, t_valid):
    bsz, tp = gq.shape[:2]
    rows = min(tp, QB)
    assert tp % rows == 0
    nc = tp // rows
    tok = lambda wd: pl.BlockSpec((1, rows, wd), lambda b, h, c: (b, c, h))
    return pl.pallas_call(
        functools.partial(_gla_kernel, t_valid=t_valid, rows=rows),
        grid=(bsz, GLA_HEADS, nc),
        in_specs=[tok(GLA_DK), tok(GLA_DK), tok(GLA_DV), tok(GLA_DV),
                  pl.BlockSpec((1, rows, LANES), lambda b, h, c: (b, c, 0)),
                  pl.BlockSpec((GLA_RANK, GLA_DK), lambda b, h, c: (0, h)),
                  pl.BlockSpec((1, GLA_DK), lambda b, h, c: (0, h)),
                  pl.BlockSpec((1, GLA_DV), lambda b, h, c: (0, 0)),
                  pl.BlockSpec((1, 1, GLA_DK, GLA_DV), lambda b, h, c: (b, h, 0, 0))],
        out_specs=[pl.BlockSpec((1, rows, GLA_DV), lambda b, h, c: (b, c, h)),
                   pl.BlockSpec((1, 1, GLA_DK, GLA_DV), lambda b, h, c: (b, h, 0, 0))],
        out_shape=[jax.ShapeDtypeStruct((bsz, tp, GLA_VW), F32),
                   jax.ShapeDtypeStruct((bsz, GLA_HEADS, GLA_DK, GLA_DV), F32)],
        scratch_shapes=[pltpu.VMEM((GLA_DK, GLA_DV), F32)],
        compiler_params=_params(("arbitrary", "arbitrary", "arbitrary")),
        name="gla",
    )(gq, gk, gv, gr, sm, wa2, ba, gain, s0)


def _rel_bucket_np(dist):
    max_exact = REL_BUCKETS // 2
    d = np.maximum(dist, 0)
    large = max_exact + (np.log(np.maximum(d, 1).astype(np.float64) / max_exact)
                         / math.log(REL_MAX_DIST / max_exact) * (REL_BUCKETS - max_exact)).astype(np.int64)
    large = np.minimum(large, REL_BUCKETS - 1)
    return np.where(d < max_exact, d, large).astype(np.int32)


BUCKET_FAR = int(_rel_bucket_np(np.array([QB]))[0])
assert (_rel_bucket_np(np.arange(QB, 4 * 8192)) == BUCKET_FAR).all()


def _to_key(x):
    bits = lax.bitcast_convert_type(x + 0.0, I32)
    return jnp.where(bits < 0, bits ^ 0x7FFFFFFF, bits)


def _bias_from_buckets(bkt, rel_ref, h):
    acc = jnp.zeros(bkt.shape, F32)
    for bb in range(REL_BUCKETS):
        acc = jnp.where(bkt == bb, rel_ref[bb, h], acc)
    return acc


def _topk_search(count, ktop, shape, nbits):
    cnt0 = count(lambda kt, pos: kt >= 0)
    tau = jnp.where(cnt0 >= ktop, jnp.zeros(shape, I32), jnp.full(shape, INT_MIN, I32))

    def bit_body(it, tau):
        cand = tau | lax.shift_left(jnp.int32(1), 30 - it)
        cnt = count(lambda kt, pos: kt >= cand)
        return jnp.where(cnt >= ktop, cand, tau)

    tau = lax.fori_loop(0, 31, bit_body, tau)
    need = ktop - count(lambda kt, pos: kt > tau)

    def pos_body(it, pc):
        cand = pc | lax.shift_left(jnp.int32(1), nbits - 1 - it)
        cnt = count(lambda kt, pos: (kt == tau) & (pos < cand))
        return jnp.where(cnt <= need, cand, pc)

    pc = lax.fori_loop(0, nbits, pos_body, jnp.zeros(shape, I32))
    return tau, pc


def _selected(kt, pos, tau, pc):
    return (kt > tau) | ((kt == tau) & (pos < pc) & (tau > INT_MIN))


def _dsa_prompt_kernel(rel_ref, bkt_ref, dq_ref, iq_ref, sm_ref, ik_ref, k_ref, vt_ref, o_ref,
                       keys_ref, madd_ref, bias_ref, qt_ref, qit_ref, *, ktop, nbits):
    bi = pl.program_id(0)
    i = pl.program_id(1)
    nkb = i + 1

    @pl.when((bi == 0) & (i == 0))
    def _():
        for h in range(DSA_HEADS):
            for t in range(2):
                bias_ref[h, t] = _bias_from_buckets(bkt_ref[t], rel_ref, h)
            bias_ref[h, 2] = jnp.full((QB, QB), rel_ref[BUCKET_FAR, h], F32)

    dq = dq_ref[0]
    for h in range(DSA_HEADS):
        qt_ref[h * DSA_HD:(h + 1) * DSA_HD, :] = dq[:, h * DSA_HD:(h + 1) * DSA_HD].T.astype(BF16)
    iq = iq_ref[0]
    for p in range(IDX_QW // LANES):
        qit_ref[p * LANES:(p + 1) * LANES, :] = iq[:, p * LANES:(p + 1) * LANES].T.astype(BF16)
    wt = sm_ref[0].T[SM_IW:SM_IW + IDX_HEADS, :] * (IDX_HEADS ** -0.5)

    s_rel = lax.broadcasted_iota(I32, (QB, QB), 0)
    q_rel = lax.broadcasted_iota(I32, (QB, QB), 1)

    def scores(j, carry):
        off = pl.multiple_of(j * QB, QB)
        kij = ik_ref[0, pl.ds(off, QB), :]
        acc = jnp.zeros((QB, QB), F32)
        for h in range(IDX_HEADS):
            s = _dot(kij, qit_ref[h * IDX_DIM:(h + 1) * IDX_DIM, :])
            acc = acc + jnp.maximum(s * (IDX_DIM ** -0.5), 0.0) * wt[h:h + 1, :]
        valid = (j * QB + s_rel) <= (i * QB + q_rel)
        keys_ref[pl.ds(off, QB), :] = jnp.where(valid, _to_key(acc), INT_MIN)
        return carry

    lax.fori_loop(0, nkb, scores, 0)

    def count(pred):
        def body(j, acc):
            off = pl.multiple_of(j * QB, QB)
            kt = keys_ref[pl.ds(off, QB), :]
            c = jnp.where(pred(kt, j * QB + s_rel), 1, 0).astype(I32)
            return acc + jnp.sum(c.reshape(QB // SUBLANES, SUBLANES, QB), axis=0)
        acc = lax.fori_loop(0, nkb, body, jnp.zeros((SUBLANES, QB), I32))
        return jnp.sum(acc, axis=0, keepdims=True)

    tau, pc = _topk_search(count, ktop, (1, QB), nbits)

    def fill_mask(j, carry):
        off = pl.multiple_of(j * QB, QB)
        kt = keys_ref[pl.ds(off, QB), :]
        sel = _selected(kt, j * QB + s_rel, tau, pc)
        madd_ref[pl.ds(off, QB), :] = jnp.where(sel, 0.0, NEG_BIG)
        return carry

    lax.fori_loop(0, nkb, fill_mask, 0)

    scale = DSA_HD ** -0.5
    for h in range(DSA_HEADS):
        g = h // (DSA_HEADS // DSA_KV_HEADS)
        qth = qt_ref[h * DSA_HD:(h + 1) * DSA_HD, :]

        def attend(j, carry):
            m, l, acc = carry
            off = pl.multiple_of(j * QB, QB)
            kj = k_ref[0, pl.ds(off, QB), g * DSA_HD:(g + 1) * DSA_HD]
            lg = (_dot(kj, qth) * scale + bias_ref[h, jnp.minimum(i - j, 2)]
                  + madd_ref[pl.ds(off, QB), :])
            mn = jnp.maximum(m, jnp.max(lg, axis=0, keepdims=True))
            alpha = jnp.exp(m - mn)
            p = jnp.exp(lg - mn)
            l = alpha * l + jnp.sum(p, axis=0, keepdims=True)
            acc = alpha * acc + _dot(vt_ref[0, g, j], p.astype(BF16))
            return mn, l, acc

        m0 = jnp.full((1, QB), NEG_BIG, F32)
        l0 = jnp.zeros((1, QB), F32)
        a0 = jnp.zeros((DSA_HD, QB), F32)
        m, l, acc = lax.fori_loop(0, nkb, attend, (m0, l0, a0))
        o_ref[0, :, h * DSA_HD:(h + 1) * DSA_HD] = (acc / l).T.astype(o_ref.dtype)


def _dsa_prompt(rel_bias, dq, iq, sm, ik_bf, k_bf, vt_bf, t_real):
    bsz, tp = dq.shape[:2]
    nb = tp // QB
    ktop = min(TOPK_MAX, t_real // 4)
    nbits = int(tp - 1).bit_length()
    s_rel = np.arange(QB)[:, None]
    q_rel = np.arange(QB)[None, :]
    bkt = np.stack([_rel_bucket_np(q_rel - s_rel), _rel_bucket_np(QB + q_rel - s_rel)])
    return pl.pallas_call(
        functools.partial(_dsa_prompt_kernel, ktop=ktop, nbits=nbits),
        grid=(bsz, nb),
        in_specs=[pl.BlockSpec(memory_space=pltpu.SMEM),
                  pl.BlockSpec((2, QB, QB), lambda b, i: (0, 0, 0)),
                  pl.BlockSpec((1, QB, DSA_QW), lambda b, i: (b, i, 0)),
                  pl.BlockSpec((1, QB, IDX_QW), lambda b, i: (b, i, 0)),
                  pl.BlockSpec((1, QB, LANES), lambda b, i: (b, i, 0)),
                  pl.BlockSpec((1, tp, IDX_DIM), lambda b, i: (b, 0, 0)),
                  pl.BlockSpec((1, tp, DSA_KVW), lambda b, i: (b, 0, 0)),
                  pl.BlockSpec((1, DSA_KV_HEADS, nb, DSA_HD, QB), lambda b, i: (b, 0, 0, 0, 0))],
        out_specs=pl.BlockSpec((1, QB, DSA_QW), lambda b, i: (b, i, 0)),
        out_shape=jax.ShapeDtypeStruct((bsz, tp, DSA_QW), F32),
        scratch_shapes=[pltpu.VMEM((tp, QB), I32),
                        pltpu.VMEM((tp, QB), F32),
                        pltpu.VMEM((DSA_HEADS, 3, QB, QB), F32),
                        pltpu.VMEM((DSA_QW, QB), BF16),
                        pltpu.VMEM((IDX_QW, QB), BF16)],
        compiler_params=_params(("arbitrary", "arbitrary")),
        name="dsa_prompt",
    )(rel_bias, jnp.asarray(bkt), dq, iq, sm, ik_bf, k_bf, vt_bf)


PP = 8


def _dsa_sample_kernel(pt_ref, rel_ref, bkt_ref, dq_ref, iq_ref, sm_ref, dk_ref, dv_ref, *rest,
                       nq, ns, ktop, nbits, past_len):
    idx_refs = rest[0:PP]
    kp_refs = rest[PP:2 * PP]
    vp_refs = rest[2 * PP:3 * PP]
    o_ref = rest[3 * PP]
    keys_ref, lg_ref, p_ref, acc_ref, linv_ref, qi_ref, qg_ref, wcol_ref = rest[3 * PP + 1:]
    step = pl.program_id(1)
    G = DSA_HEADS // DSA_KV_HEADS
    GR = G * nq
    HR = DSA_HEADS * nq
    SW = PP * PAGE_SIZE
    scale = DSA_HD ** -0.5

    @pl.when(step == 0)
    def _():
        iq = iq_ref[...]
        dq = dq_ref[...]
        sm = sm_ref[...]
        qi_ref[...] = jnp.concatenate(
            [iq[:, h * IDX_DIM:(h + 1) * IDX_DIM] for h in range(IDX_HEADS)], axis=0).astype(BF16)
        qg_ref[...] = jnp.concatenate(
            [dq[:, h * DSA_HD:(h + 1) * DSA_HD] for h in range(DSA_HEADS)], axis=0).astype(BF16)
        wcol = jnp.concatenate(
            [sm[:, SM_IW + h:SM_IW + h + 1] for h in range(IDX_HEADS)], axis=0) * (IDX_HEADS ** -0.5)
        wcol_ref[...] = jnp.broadcast_to(wcol, (HR, LANES))

    def idx_scores(kib):
        s = _dot_nt(qi_ref[...], kib)
        s = jnp.maximum(s * (IDX_DIM ** -0.5), 0.0) * wcol_ref[...]
        return jnp.sum(s.reshape(IDX_HEADS, nq, s.shape[-1]), axis=0)

    def logits(kb):
        return jnp.concatenate(
            [_dot_nt(qg_ref[g * GR:(g + 1) * GR, :], kb[:, g * DSA_HD:(g + 1) * DSA_HD])
             for g in range(DSA_KV_HEADS)], axis=0) * scale

    @pl.when(step < ns)
    def _():
        for pi in range(PP):
            lo, hi = pi * PAGE_SIZE, (pi + 1) * PAGE_SIZE
            keys_ref[step, :, lo:hi] = _to_key(idx_scores(idx_refs[pi][0].astype(BF16)))
            lg_ref[step, :, lo:hi] = logits(kp_refs[pi][0].astype(BF16))

    @pl.when(step == ns - 1)
    def _():
        zpad = lambda z: jnp.concatenate(
            [z, jnp.zeros((PAGE_SIZE - nq, z.shape[1]), z.dtype)], axis=0)
        sm = sm_ref[...]
        n_idx = lax.broadcasted_iota(I32, (nq, PAGE_SIZE), 1)
        q_idx = lax.broadcasted_iota(I32, (nq, PAGE_SIZE), 0)
        sc_new = idx_scores(zpad(sm[:, SM_IK:SM_IK + IDX_DIM]).astype(BF16))
        keys_ref[ns, :, 0:PAGE_SIZE] = jnp.where(n_idx <= q_idx, _to_key(sc_new), INT_MIN)
        lg_ref[ns, :, 0:PAGE_SIZE] = logits(zpad(dk_ref[...]).astype(BF16))

        lane = lax.broadcasted_iota(I32, (nq, SW), 1)

        def count(pred):
            acc = jnp.zeros((nq, PAGE_SIZE), I32)
            for s in range(ns):
                c = jnp.where(pred(keys_ref[s], s * SW + lane), 1, 0).astype(I32)
                for t in range(PP):
                    acc = acc + c[:, t * PAGE_SIZE:(t + 1) * PAGE_SIZE]
            acc = acc + jnp.where(pred(keys_ref[ns, :, 0:PAGE_SIZE], past_len + n_idx), 1, 0).astype(I32)
            return jnp.sum(acc, axis=1, keepdims=True)

        tau, pc = _topk_search(count, ktop, (nq, 1), nbits)

        far = jnp.concatenate([jnp.full((nq, 1), rel_ref[BUCKET_FAR, h], F32)
                               for h in range(DSA_HEADS)], axis=0)
        near = [jnp.concatenate([_bias_from_buckets(bkt_ref[t], rel_ref, h)
                                 for h in range(DSA_HEADS)], axis=0) for t in range(2)]
        tile8 = lambda z: jnp.concatenate([z] * DSA_HEADS, axis=0)

        m = jnp.full((HR, 1), NEG_BIG, F32)
        for s in range(ns + 1):
            if s < ns:
                sel = _selected(keys_ref[s], s * SW + lane, tau, pc)
                bias = far
                lg = lg_ref[s] + tile8(jnp.where(sel, 0.0, NEG_BIG))
                if s == ns - 1:
                    lg = jnp.concatenate(
                        [lg[:, :SW - PAGE_SIZE] + far, lg[:, SW - PAGE_SIZE:] + near[0]], axis=1)
                else:
                    lg = lg + bias
                lg_ref[s] = lg
            else:
                sel = _selected(keys_ref[ns, :, 0:PAGE_SIZE], past_len + n_idx, tau, pc)
                lg = lg_ref[ns, :, 0:PAGE_SIZE] + tile8(jnp.where(sel, 0.0, NEG_BIG)) + near[1]
                lg_ref[ns, :, 0:PAGE_SIZE] = lg
            m = jnp.maximum(m, jnp.max(lg, axis=1, keepdims=True))

        l = jnp.zeros((HR, 1), F32)
        for s in range(ns):
            p = jnp.exp(lg_ref[s] - m)
            l = l + jnp.sum(p, axis=1, keepdims=True)
            p_ref[s] = p.astype(BF16)
        pn = jnp.exp(lg_ref[ns, :, 0:PAGE_SIZE] - m)
        l = l + jnp.sum(pn, axis=1, keepdims=True)
        linv_ref[...] = jnp.broadcast_to(1.0 / l, (HR, LANES))
        vn = zpad(dv_ref[...]).astype(BF16)
        pnb = pn.astype(BF16)
        for g in range(DSA_KV_HEADS):
            acc_ref[g * GR:(g + 1) * GR, :] = _dot(pnb[g * GR:(g + 1) * GR, :],
                                                   vn[:, g * DSA_HD:(g + 1) * DSA_HD])

    @pl.when(step >= ns)
    def _():
        s = step - ns
        for pi in range(PP):
            vb = vp_refs[pi][0].astype(BF16)
            for g in range(DSA_KV_HEADS):
                acc_ref[g * GR:(g + 1) * GR, :] += _dot(
                    p_ref[s, g * GR:(g + 1) * GR, pi * PAGE_SIZE:(pi + 1) * PAGE_SIZE],
                    vb[:, g * DSA_HD:(g + 1) * DSA_HD])

    @pl.when(step == 2 * ns - 1)
    def _():
        out = acc_ref[...] * linv_ref[...]
        for h in range(DSA_HEADS):
            o_ref[:, h * DSA_HD:(h + 1) * DSA_HD] = out[h * nq:(h + 1) * nq, :]


def _dsa_sample(page_table, rel_bias, dq, iq, sm, dk, dv, cache_idx, cache_k, cache_v, layer, n_pool):
    db, n_pages = page_table.shape
    nq = dq.shape[0] // db
    assert nq == SUBLANES and n_pages % PP == 0 and nq <= PAGE_SIZE
    ns = n_pages // PP
    past_len = n_pages * PAGE_SIZE
    ktop = min(TOPK_MAX, (past_len + nq) // 4)
    nbits = int(past_len + nq - 1).bit_length()
    base = layer * n_pool
    q_idx = np.arange(nq)[:, None]
    lane = np.arange(PAGE_SIZE)[None, :]
    bkt = np.stack([_rel_bucket_np(PAGE_SIZE + q_idx - lane), _rel_bucket_np(q_idx - lane)])
    hr = DSA_HEADS * nq

    def page_spec(width, phase, pi):
        if phase == 0:
            fn = lambda b, s, pt: (base + pt[b, jnp.minimum(s, ns - 1) * PP + pi], 0, 0)
        else:
            fn = lambda b, s, pt: (base + pt[b, jnp.maximum(s - ns, 0) * PP + pi], 0, 0)
        return pl.BlockSpec((1, PAGE_SIZE, width), fn)

    tok = lambda wd: pl.BlockSpec((nq, wd), lambda b, s, pt: (b, 0))
    grid_spec = pltpu.PrefetchScalarGridSpec(
        num_scalar_prefetch=1,
        grid=(db, 2 * ns),
        in_specs=([pl.BlockSpec(memory_space=pltpu.SMEM),
                   pl.BlockSpec((2, nq, PAGE_SIZE), lambda b, s, pt: (0, 0, 0)),
                   tok(DSA_QW), tok(IDX_QW), tok(LANES), tok(DSA_KVW), tok(DSA_KVW)]
                  + [page_spec(IDX_DIM, 0, pi) for pi in range(PP)]
                  + [page_spec(DSA_KVW, 0, pi) for pi in range(PP)]
                  + [page_spec(DSA_KVW, 1, pi) for pi in range(PP)]),
        out_specs=tok(DSA_QW),
        scratch_shapes=[pltpu.VMEM((ns + 1, nq, PP * PAGE_SIZE), I32),
                        pltpu.VMEM((ns + 1, hr, PP * PAGE_SIZE), F32),
                        pltpu.VMEM((ns, hr, PP * PAGE_SIZE), BF16),
                        pltpu.VMEM((hr, DSA_HD), F32),
                        pltpu.VMEM((hr, LANES), F32),
                        pltpu.VMEM((hr, IDX_DIM), BF16),
                        pltpu.VMEM((hr, DSA_HD), BF16),
                        pltpu.VMEM((hr, LANES), F32)])
    return pl.pallas_call(
        functools.partial(_dsa_sample_kernel, nq=nq, ns=ns, ktop=ktop, nbits=nbits, past_len=past_len),
        grid_spec=grid_spec,
        out_shape=jax.ShapeDtypeStruct((db * nq, DSA_QW), F32),
        compiler_params=_params(("arbitrary", "arbitrary")),
        name="dsa_sample",
    )(page_table, rel_bias, jnp.asarray(bkt), dq, iq, sm, dk, dv,
      *([cache_idx] * PP), *([cache_k] * PP), *([cache_v] * PP))


def _merge_kernel(x_ref, oa_ref, ob_ref, ga_ref, gb_ref, wpa_ref, wpb_ref, wout_ref, gpost_ref, o_ref):
    pa = _dot(oa_ref[...].astype(BF16), wpa_ref[...])
    pb = _dot(ob_ref[...].astype(BF16), wpb_ref[...])
    m = _sigmoid(ga_ref[...]) * pa + _sigmoid(gb_ref[...]) * pb
    mo = _dot(m.astype(BF16), wout_ref[...])
    o_ref[...] = x_ref[...] + _rms(mo, gpost_ref[...])


def _merge(x, oa, ob, ga, gb, wpa, wpb, wout, gpost, tm):
    n = x.shape[0]
    assert n % tm == 0
    row = pl.BlockSpec((tm, D_MODEL), lambda i: (i, 0))
    wsp = pl.BlockSpec((D_MODEL, D_MODEL), lambda i: (0, 0))
    return pl.pallas_call(
        _merge_kernel,
        grid=(n // tm,),
        in_specs=[row, row, row, row, row, wsp, wsp, wsp, pl.BlockSpec((1, D_MODEL), lambda i: (0, 0))],
        out_specs=row,
        out_shape=jax.ShapeDtypeStruct((n, D_MODEL), F32),
        compiler_params=_params(("arbitrary",)),
        name="merge",
    )(x, oa, ob, ga, gb, wpa, wpb, wout, gpost)


FFN_TF = 1024
PREV_ROWS = 16


def _ffn_kernel(x_ref, xprev_ref, gpre_ref, gpost_ref, wg_ref, wu_ref, cw_ref, cb_ref, wd_ref, st_ref,
                o_ref, tail_ref, h2_ref, hp_ref, gext_ref, acc_ref,
                *, shift, nprev, tiles_per_seq, use_state, tail_off):
    i = pl.program_id(0)
    j = pl.program_id(1)
    tm = x_ref.shape[0]

    @pl.when(j == 0)
    def _():
        h2_ref[...] = _rms(x_ref[...], gpre_ref[...]).astype(BF16)
        hp_ref[...] = _rms(xprev_ref[...], gpre_ref[...]).astype(BF16)
        acc_ref[...] = jnp.zeros_like(acc_ref)

    h2 = h2_ref[...]
    g = _dot(h2, wg_ref[...])
    u = _dot(h2, wu_ref[...])
    if use_state:
        gprev = st_ref[...]
    else:
        gprev = _dot(hp_ref[...], wg_ref[...])
        gprev = jnp.where(i % tiles_per_seq == 0, 0.0, gprev)
    gext_ref[0:nprev, :] = gprev
    gext_ref[nprev:nprev + tm, :] = g
    cw = cw_ref[...]
    c = (cb_ref[...] + gext_ref[nprev - 2 * shift:nprev - 2 * shift + tm, :] * cw[0:1, :]
         + gext_ref[nprev - shift:nprev - shift + tm, :] * cw[1:2, :] + g * cw[2:3, :])
    gelu = 0.5 * c * (1.0 + jnp.tanh(math.sqrt(2.0 / math.pi) * (c + 0.044715 * (c * c * c))))
    acc_ref[...] += _dot((gelu * u).astype(BF16), wd_ref[...])
    tail_ref[...] = g[tail_off:tail_off + tail_ref.shape[0], :]

    @pl.when(j == pl.num_programs(1) - 1)
    def _():
        o_ref[...] = x_ref[...] + _rms(acc_ref[...], gpost_ref[...])


def _ffn(x, gpre, gpost, w_in_bf, conv_w, conv_b, w_down_bf, state, *, tm, shift, tiles_per_seq,
         use_state, tail_off, tail_rows):
    n = x.shape[0]
    assert n % tm == 0 and tm % PREV_ROWS == 0 and D_FF % FFN_TF == 0
    nprev = state.shape[0] if use_state else PREV_ROWS
    assert nprev >= 2 * shift and (not use_state or n == tm)
    nj = D_FF // FFN_TF
    pr = tm // PREV_ROWS
    return pl.pallas_call(
        functools.partial(_ffn_kernel, shift=shift, nprev=nprev, tiles_per_seq=tiles_per_seq,
                          use_state=use_state, tail_off=tail_off),
        grid=(n // tm, nj),
        in_specs=[pl.BlockSpec((tm, D_MODEL), lambda i, j: (i, 0)),
                  pl.BlockSpec((PREV_ROWS, D_MODEL), lambda i, j: (jnp.maximum(i * pr - 1, 0), 0)),
                  pl.BlockSpec((1, D_MODEL), lambda i, j: (0, 0)),
                  pl.BlockSpec((1, D_MODEL), lambda i, j: (0, 0)),
                  pl.BlockSpec((D_MODEL, FFN_TF), lambda i, j: (0, j)),
                  pl.BlockSpec((D_MODEL, FFN_TF), lambda i, j: (0, j + nj)),
                  pl.BlockSpec((CONV_W, FFN_TF), lambda i, j: (0, j)),
                  pl.BlockSpec((1, FFN_TF), lambda i, j: (0, j)),
                  pl.BlockSpec((FFN_TF, D_MODEL), lambda i, j: (j, 0)),
                  pl.BlockSpec((state.shape[0], FFN_TF), lambda i, j: (0, j))],
        out_specs=[pl.BlockSpec((tm, D_MODEL), lambda i, j: (i, 0)),
                   pl.BlockSpec((tail_rows, FFN_TF), lambda i, j: (i, j))],
        out_shape=[jax.ShapeDtypeStruct((n, D_MODEL), F32),
                   jax.ShapeDtypeStruct((n // tm * tail_rows, D_FF), F32)],
        scratch_shapes=[pltpu.VMEM((tm, D_MODEL), BF16),
                        pltpu.VMEM((PREV_ROWS, D_MODEL), BF16),
                        pltpu.VMEM((nprev + tm, FFN_TF), F32),
                        pltpu.VMEM((tm, D_MODEL), F32)],
        compiler_params=_params(("arbitrary", "arbitrary")),
        name="ffn",
    )(x, x, gpre, gpost, w_in_bf, w_in_bf, conv_w, conv_b, w_down_bf, state)


def _round_up(a, m):
    return -(-a // m) * m


def kernel(x_prompt, x_sample, cache_k, cache_v, cache_idx_k, state_gla, state_conv, page_table, meta_tokens, rel_bias, norm_mix_pre, norm_mix_post, norm_ffn_pre, norm_ffn_post, w_in, w_a2, b_a, gla_norm, w_pa, w_pb, w_out, w_ffn_in, conv_w, conv_b, w_ffn_down):
    bsz, seq = x_prompt.shape[:2]
    db, nq = x_sample.shape[:2]
    depth = w_in.shape[0]
    n_pool = cache_k.shape[1]
    t_real = seq + N_META
    tp = _round_up(t_real, QB)
    nb = tp // QB
    row2 = lambda a: a.reshape(1, -1)

    w_in_r = [_relayout_w_in(w_in[l]) for l in range(depth)]
    w_pa_b, w_pb_b, w_out_b = w_pa.astype(BF16), w_pb.astype(BF16), w_out.astype(BF16)
    w_ffn_in_b, w_ffn_down_b = w_ffn_in.astype(BF16), w_ffn_down.astype(BF16)

    ck = cache_k.reshape(depth * n_pool, PAGE_SIZE, DSA_KVW)
    cv = cache_v.reshape(depth * n_pool, PAGE_SIZE, DSA_KVW)
    ci = cache_idx_k.reshape(depth * n_pool, PAGE_SIZE, IDX_DIM)

    xp = jnp.concatenate([jnp.broadcast_to(meta_tokens[None].astype(x_prompt.dtype), (bsz, N_META, D_MODEL)),
                          x_prompt, jnp.zeros((bsz, tp - t_real, D_MODEL), x_prompt.dtype)], axis=1)
    xp = xp.reshape(bsz * tp, D_MODEL)
    ffn_tm = tp // 4 if (tp // 4) % PREV_ROWS == 0 else tp
    tiles_per_seq = tp // ffn_tm
    tail_pos = (t_real - (CONV_W - 1)) % ffn_tm
    tail_off = tail_pos // SUBLANES * SUBLANES
    assert tail_pos - tail_off + (CONV_W - 1) <= SUBLANES
    pk, pv, pik, pgla, pconv = [], [], [], [], []
    zero_state = jnp.zeros((bsz, GLA_HEADS, GLA_DK, GLA_DV), F32)
    zero_conv = jnp.zeros((PREV_ROWS, D_FF), F32)
    for l in range(depth):
        z = dict(zip([n for n, _ in SEGS], _proj(xp, row2(norm_mix_pre[l]), w_in_r[l], 256)))
        r3 = lambda a: a.reshape(bsz, tp, a.shape[-1])
        o_a, s_fin = _gla(r3(z["gq"]), r3(z["gk"]), r3(z["gv"]), r3(z["gr"]), r3(z["sm"]),
                          w_a2[l], row2(b_a[l]), row2(gla_norm[l]), zero_state, t_real)
        dk3, dv3, sm3 = r3(z["dk"]), r3(z["dv"]), r3(z["sm"])
        vt = dv3.astype(BF16).reshape(bsz, nb, QB, DSA_KV_HEADS, DSA_HD).transpose(0, 3, 1, 4, 2)
        o_b = _dsa_prompt(rel_bias, r3(z["dq"]), r3(z["iq"]), sm3,
                          sm3[:, :, SM_IK:SM_IK + IDX_DIM].astype(BF16), dk3.astype(BF16), vt, t_real)
        xm = _merge(xp, o_a.reshape(bsz * tp, GLA_VW), o_b.reshape(bsz * tp, DSA_QW), z["ga"], z["gb"],
                    w_pa_b[l], w_pb_b[l], w_out_b[l], row2(norm_mix_post[l]), 512 if (bsz * tp) % 512 == 0 else QB)
        xp, tail = _ffn(xm, row2(norm_ffn_pre[l]), row2(norm_ffn_post[l]), w_ffn_in_b[l], conv_w[l],
                        row2(conv_b[l]), w_ffn_down_b[l], zero_conv, tm=ffn_tm, shift=1,
                        tiles_per_seq=tiles_per_seq, use_state=False, tail_off=tail_off, tail_rows=SUBLANES)
        pk.append(dk3[:, :t_real].reshape(bsz, t_real, DSA_KV_HEADS, DSA_HD))
        pv.append(dv3[:, :t_real].reshape(bsz, t_real, DSA_KV_HEADS, DSA_HD))
        pik.append(sm3[:, :t_real, SM_IK:SM_IK + IDX_DIM])
        pgla.append(s_fin)
        tail = tail.reshape(bsz, tiles_per_seq, SUBLANES, D_FF)[:, (t_real - 1) // ffn_tm]
        pconv.append(tail[:, tail_pos - tail_off:tail_pos - tail_off + CONV_W - 1])
    y_prompt = xp.reshape(bsz, tp, D_MODEL)[:, N_META:t_real]

    ns_rows = db * nq
    xs = x_sample.reshape(ns_rows, D_MODEL)
    sk, sv, sik, sgla, sconv = [], [], [], [], []
    for l in range(depth):
        z = dict(zip([n for n, _ in SEGS], _proj(xs, row2(norm_mix_pre[l]), w_in_r[l], ns_rows)))
        r3 = lambda a: a.reshape(db, nq, a.shape[-1])
        o_a, s_fin = _gla(r3(z["gq"]), r3(z["gk"]), r3(z["gv"]), r3(z["gr"]), r3(z["sm"]),
                          w_a2[l], row2(b_a[l]), row2(gla_norm[l]), state_gla[l], nq)
        o_b = _dsa_sample(page_table, rel_bias, z["dq"], z["iq"], z["sm"], z["dk"], z["dv"],
                          ci, ck, cv, l, n_pool)
        xm = _merge(xs, o_a.reshape(ns_rows, GLA_VW), o_b, z["ga"], z["gb"],
                    w_pa_b[l], w_pb_b[l], w_out_b[l], row2(norm_mix_post[l]), ns_rows)
        xm_t = xm.reshape(db, nq, D_MODEL).transpose(1, 0, 2).reshape(ns_rows, D_MODEL)
        st = state_conv[l].transpose(1, 0, 2).reshape((CONV_W - 1) * db, D_FF)
        xo_t, tail = _ffn(xm_t, row2(norm_ffn_pre[l]), row2(norm_ffn_post[l]), w_ffn_in_b[l], conv_w[l],
                          row2(conv_b[l]), w_ffn_down_b[l], st, tm=ns_rows, shift=db, tiles_per_seq=1,
                          use_state=True, tail_off=(nq - (CONV_W - 1)) * db, tail_rows=(CONV_W - 1) * db)
        xs = xo_t.reshape(nq, db, D_MODEL).transpose(1, 0, 2).reshape(ns_rows, D_MODEL)
        sk.append(z["dk"].reshape(db, nq, DSA_KV_HEADS, DSA_HD))
        sv.append(z["dv"].reshape(db, nq, DSA_KV_HEADS, DSA_HD))
        sik.append(z["sm"][:, SM_IK:SM_IK + IDX_DIM].reshape(db, nq, IDX_DIM))
        sgla.append(s_fin)
        sconv.append(tail.reshape(CONV_W - 1, db, D_FF).transpose(1, 0, 2))
    y_sample = xs.reshape(db, nq, D_MODEL)

    return (y_prompt, y_sample,
            jnp.stack(pk), jnp.stack(pv), jnp.stack(pik), jnp.stack(pgla), jnp.stack(pconv),
            jnp.stack(sk), jnp.stack(sv), jnp.stack(sik), jnp.stack(sgla), jnp.stack(sconv))
```

```python
import functools
import math

import jax
import jax.numpy as jnp
import numpy as np
from jax import lax
from jax.experimental import pallas as pl
from jax.experimental.pallas import tpu as pltpu

F32 = jnp.float32
BF16 = jnp.bfloat16
I32 = jnp.int32

D_MODEL = 1024
N_META = 16
GLA_HEADS = 4
GLA_DK = 128
GLA_DV = 256
GLA_RANK = 16
GLA_TAU = 16.0
DSA_HEADS = 8
DSA_KV_HEADS = 2
DSA_HD = 128
IDX_HEADS = 8
IDX_DIM = 64
TOPK_MAX = 256
REL_BUCKETS = 32
REL_MAX_DIST = 128
D_FF = 4096
CONV_W = 3
EPS = 1e-6
PAGE_SIZE = 128

GLA_KW = GLA_HEADS * GLA_DK
GLA_VW = GLA_HEADS * GLA_DV
DSA_QW = DSA_HEADS * DSA_HD
DSA_KVW = DSA_KV_HEADS * DSA_HD
IDX_QW = IDX_HEADS * IDX_DIM

LANES = 128
SUBLANES = 8
QB = 128
VMEM_LIMIT = 56 * 1024 * 1024

INT_MIN = -(2 ** 31)
NEG_BIG = -1e30

SEGS = (("gq", GLA_KW), ("gk", GLA_KW), ("gv", GLA_VW), ("gr", GLA_VW), ("dq", DSA_QW),
        ("dk", DSA_KVW), ("dv", DSA_KVW), ("iq", IDX_QW), ("ga", D_MODEL), ("gb", D_MODEL),
        ("sm", LANES))
SM_IK = 0
SM_GA1 = IDX_DIM
SM_IW = IDX_DIM + GLA_RANK
SEG_OFFS = tuple(int(v) for v in np.cumsum([0] + [w for _, w in SEGS]))
PW = SEG_OFFS[-1]


def _relayout_w_in(w):
    o = np.cumsum([0, GLA_KW, GLA_KW, GLA_VW, GLA_VW, GLA_RANK, DSA_QW, DSA_KVW, DSA_KVW,
                   IDX_QW, IDX_HEADS, IDX_DIM, D_MODEL, D_MODEL]).tolist()
    c = lambda i: w[:, o[i]:o[i + 1]]
    pad = jnp.zeros((w.shape[0], LANES - IDX_DIM - GLA_RANK - IDX_HEADS), w.dtype)
    cols = [c(0), c(1), c(2), c(3), c(5), c(6), c(7), c(8), c(11), c(12), c(10), c(4), c(9), pad]
    return jnp.concatenate(cols, axis=1).astype(BF16)


def _rms(x, g):
    return x * lax.rsqrt(jnp.mean(x * x, axis=-1, keepdims=True) + EPS) * g


def _sigmoid(x):
    return 1.0 / (1.0 + jnp.exp(-x))


def _dot(a, b):
    return jnp.dot(a, b, preferred_element_type=F32)


def _dot_nt(a, b):
    return lax.dot_general(a, b, (((1,), (1,)), ((), ())), preferred_element_type=F32)


def _params(sem):
    return pltpu.CompilerParams(dimension_semantics=sem, vmem_limit_bytes=VMEM_LIMIT)


def _proj_kernel(x_ref, g_ref, w_ref, *out_refs):
    hb = _rms(x_ref[...], g_ref[...]).astype(BF16)
    for o_ref, off in zip(out_refs, SEG_OFFS[:-1]):
        wd = o_ref.shape[-1]
        o_ref[...] = _dot(hb, w_ref[:, off:off + wd])


def _proj(x, g, w, tm):
    n = x.shape[0]
    assert n % tm == 0
    return pl.pallas_call(
        _proj_kernel,
        grid=(n // tm,),
        in_specs=[pl.BlockSpec((tm, D_MODEL), lambda i: (i, 0)),
                  pl.BlockSpec((1, D_MODEL), lambda i: (0, 0)),
                  pl.BlockSpec((D_MODEL, PW), lambda i: (0, 0))],
        out_specs=[pl.BlockSpec((tm, wd), lambda i: (i, 0)) for _, wd in SEGS],
        out_shape=[jax.ShapeDtypeStruct((n, wd), F32) for _, wd in SEGS],
        compiler_params=_params(("arbitrary",)),
        name="proj",
    )(x, g, w)


def _gla_kernel(q_ref, k_ref, v_ref, gr_ref, sm_ref, wa2_ref, ba_ref, gain_ref, s0_ref,
                o_ref, sfin_ref, s_ref, *, t_valid, rows):
    C = QB
    c = pl.program_id(2)

    @pl.when(c == 0)
    def _():
        s_ref[...] = s0_ref[0, 0]

    def padrows(z):
        if rows == C:
            return z
        return jnp.concatenate([z, jnp.zeros((C - rows, z.shape[1]), z.dtype)], axis=0)

    rowi = lax.broadcasted_iota(I32, (C, 1), 0)
    live = c * C + rowi < t_valid
    q = padrows(q_ref[0]) * (GLA_DK ** -0.5)
    k = jnp.where(live, padrows(k_ref[0]), 0.0)
    v = padrows(v_ref[0])
    ga1 = padrows(sm_ref[0])[:, SM_GA1:SM_GA1 + GLA_RANK]
    x = jnp.dot(ga1, wa2_ref[...], precision=lax.Precision.HIGHEST,
                preferred_element_type=F32) + ba_ref[...]
    la = (jnp.minimum(x, 0.0) - jnp.log1p(jnp.exp(-jnp.abs(x)))) * (1.0 / GLA_TAU)
    la = jnp.where(live, la, 0.0)
    ti = lax.broadcasted_iota(I32, (C, C), 0)
    si = lax.broadcasted_iota(I32, (C, C), 1)
    tri = (ti >= si).astype(F32)
    b = jnp.dot(tri, la, precision=lax.Precision.HIGHEST, preferred_element_type=F32)

    att = None
    m = C // 2
    while m >= SUBLANES:
        pieces = []
        for p in range(C // (2 * m)):
            r = p * 2 * m + m - 1
            pieces.append(jnp.broadcast_to(b[r:r + 1, :], (2 * m, GLA_DK)))
        bound = pieces[0] if len(pieces) == 1 else jnp.concatenate(pieces, axis=0)
        upper = ((rowi // m) % 2) == 1
        d = b - bound
        qm = jnp.where(upper, q * jnp.exp(jnp.minimum(d, 0.0)), 0.0)
        km = jnp.where(upper, 0.0, k * jnp.exp(jnp.minimum(-d, 0.0)))
        a = _dot_nt(qm.astype(BF16), km.astype(BF16))
        if 2 * m < C:
            a = jnp.where((ti // (2 * m)) == (si // (2 * m)), a, 0.0)
        att = a if att is None else att + a
        m //= 2

    tm8 = rowi % SUBLANES
    intra_d = jnp.zeros((C, GLA_DV), F32)
    for delta in range(SUBLANES):
        if delta == 0:
            kd, bd, vd = k, b, v
        else:
            kd = pltpu.roll(k, delta, 0)
            bd = pltpu.roll(b, delta, 0)
            vd = pltpu.roll(v, delta, 0)
        w = q * kd * jnp.exp(jnp.minimum(b - bd, 0.0))
        a = jnp.sum(w, axis=1, keepdims=True)
        a = jnp.where(tm8 >= delta, a, 0.0)
        intra_d = intra_d + a * vd

    s = s_ref[...]
    vb = v.astype(BF16)
    inter = _dot((q * jnp.exp(b)).astype(BF16), s.astype(BF16))
    o = inter + _dot(att.astype(BF16), vb) + intra_d

    b_last_row = b[C - 1:C, :]
    b_last_col = b.T[:, C - 1:C]
    kdec = k * jnp.exp(b_last_row - b)
    s_new = s * jnp.exp(b_last_col) + _dot(kdec.T.astype(BF16), vb)
    s_ref[...] = s_new

    on = o * lax.rsqrt(jnp.mean(o * o, axis=-1, keepdims=True) + EPS) * gain_ref[...]
    gr = padrows(gr_ref[0])
    res = on * (gr * _sigmoid(gr))
    o_ref[0] = res[:rows].astype(o_ref.dtype)

    @pl.when(c == pl.num_programs(2) - 1)
    def _():
        sfin_ref[0, 0] = s_new


def _gla(gq, gk, gv, gr, sm, wa2, ba, gain, s0, t_valid):
    bsz, tp = gq.shape[:2]
    rows = min(tp, QB)
    assert tp % rows == 0
    nc = tp // rows
    tok = lambda wd: pl.BlockSpec((1, rows, wd), lambda b, h, c: (b, c, h))
    return pl.pallas_call(
        functools.partial(_gla_kernel, t_valid=t_valid, rows=rows),
        grid=(bsz, GLA_HEADS, nc),
        in_specs=[tok(GLA_DK), tok(GLA_DK), tok(GLA_DV), tok(GLA_DV),
                  pl.BlockSpec((1, rows, LANES), lambda b, h, c: (b, c, 0)),
                  pl.BlockSpec((GLA_RANK, GLA_DK), lambda b, h, c: (0, h)),
                  pl.BlockSpec((1, GLA_DK), lambda b, h, c: (0, h)),
                  pl.BlockSpec((1, GLA_DV), lambda b, h, c: (0, 0)),
                  pl.BlockSpec((1, 1, GLA_DK, GLA_DV), lambda b, h, c: (b, h, 0, 0))],
        out_specs=[pl.BlockSpec((1, rows, GLA_DV), lambda b, h, c: (b, c, h)),
                   pl.BlockSpec((1, 1, GLA_DK, GLA_DV), lambda b, h, c: (b, h, 0, 0))],
        out_shape=[jax.ShapeDtypeStruct((bsz, tp, GLA_VW), F32),
                   jax.ShapeDtypeStruct((bsz, GLA_HEADS, GLA_DK, GLA_DV), F32)],
        scratch_shapes=[pltpu.VMEM((GLA_DK, GLA_DV), F32)],
        compiler_params=_params(("arbitrary", "arbitrary", "arbitrary")),
        name="gla",
    )(gq, gk, gv, gr, sm, wa2, ba, gain, s0)


def _rel_bucket_np(dist):
    max_exact = REL_BUCKETS // 2
    d = np.maximum(dist, 0)
    large = max_exact + (np.log(np.maximum(d, 1).astype(np.float64) / max_exact)
                         / math.log(REL_MAX_DIST / max_exact) * (REL_BUCKETS - max_exact)).astype(np.int64)
    large = np.minimum(large, REL_BUCKETS - 1)
    return np.where(d < max_exact, d, large).astype(np.int32)


BUCKET_FAR = int(_rel_bucket_np(np.array([QB]))[0])
assert (_rel_bucket_np(np.arange(QB, 4 * 8192)) == BUCKET_FAR).all()


def _to_key(x):
    bits = lax.bitcast_convert_type(x + 0.0, I32)
    return jnp.where(bits < 0, bits ^ 0x7FFFFFFF, bits)


def _bias_from_buckets(bkt, rel_ref, h):
    acc = jnp.zeros(bkt.shape, F32)
    for bb in range(REL_BUCKETS):
        acc = jnp.where(bkt == bb, rel_ref[bb, h], acc)
    return acc


def _topk_search(count, ktop, shape, nbits):
    def pending(fin):
        return jnp.max(jnp.where(fin > 0, 0.0, 1.0)) > 0.0

    n_valid = count(lambda kt, pos: kt > INT_MIN)
    cnt0 = count(lambda kt, pos: kt >= 0)
    tau = jnp.where(cnt0 >= ktop, jnp.zeros(shape, I32), jnp.full(shape, INT_MIN, I32))
    fin = ((n_valid <= ktop) | (cnt0 == ktop)).astype(I32)

    def bit_cond(st):
        it, _, fin = st
        return (it < 31) & pending(fin)

    def bit_body(st):
        it, tau, fin = st
        cand = tau | lax.shift_left(jnp.int32(1), 30 - it)
        cnt = count(lambda kt, pos: kt >= cand)
        return it + 1, jnp.where(cnt >= ktop, cand, tau), fin | (cnt == ktop).astype(I32)

    _, tau, fin = lax.while_loop(bit_cond, bit_body, (jnp.int32(0), tau, fin))

    def tie_search():
        need = ktop - count(lambda kt, pos: kt > tau)

        def pos_body(it, pc):
            cand = pc | lax.shift_left(jnp.int32(1), nbits - 1 - it)
            cnt = count(lambda kt, pos: (kt == tau) & (pos < cand))
            return jnp.where(cnt <= need, cand, pc)

        return lax.fori_loop(0, nbits, pos_body, jnp.zeros(shape, I32))

    pc = lax.cond(pending(fin), tie_search, lambda: jnp.full(shape, (1 << nbits) - 1, I32))
    return tau, pc


def _selected(kt, pos, tau, pc):
    return (kt > tau) | ((kt == tau) & (pos < pc) & (tau > INT_MIN))


def _dsa_prompt_kernel(rel_ref, bkt_ref, dq_ref, iq_ref, sm_ref, ik_ref, k_ref, vt_ref, o_ref,
                       keys_ref, madd_ref, bias_ref, qt_ref, qit_ref, m_ref, l_ref, acc_ref,
                       *, ktop, nbits):
    bi = pl.program_id(0)
    i = pl.program_id(1)
    nkb = i + 1
    G = DSA_HEADS // DSA_KV_HEADS
    hcols = lambda hh: slice(hh * QB, (hh + 1) * QB)

    @pl.when((bi == 0) & (i == 0))
    def _():
        for h in range(DSA_HEADS):
            g, hh = divmod(h, G)
            for t in range(2):
                bias_ref[t, g, :, hcols(hh)] = _bias_from_buckets(bkt_ref[t], rel_ref, h)
            bias_ref[2, g, :, hcols(hh)] = jnp.full((QB, QB), rel_ref[BUCKET_FAR, h], F32)

    dq = dq_ref[0]
    for h in range(DSA_HEADS):
        g, hh = divmod(h, G)
        qt_ref[g, :, hcols(hh)] = dq[:, h * DSA_HD:(h + 1) * DSA_HD].T.astype(BF16)
    iq = iq_ref[0]
    per = LANES // IDX_DIM
    for p in range(IDX_QW // LANES):
        t = iq[:, p * LANES:(p + 1) * LANES].T
        for r in range(per):
            qit_ref[:, hcols(p * per + r)] = t[r * IDX_DIM:(r + 1) * IDX_DIM].astype(BF16)
    smt = sm_ref[0].T
    wrow = jnp.concatenate([smt[SM_IW + h:SM_IW + h + 1, :] for h in range(IDX_HEADS)],
                           axis=1) * (IDX_HEADS ** -0.5)

    s_rel = lax.broadcasted_iota(I32, (QB, QB), 0)
    q_rel = lax.broadcasted_iota(I32, (QB, QB), 1)

    def scores(j, carry):
        off = pl.multiple_of(j * QB, QB)
        s = _dot(ik_ref[0, pl.ds(off, QB), :], qit_ref[...])
        s = jnp.maximum(s * (IDX_DIM ** -0.5), 0.0) * wrow
        acc = s[:, hcols(0)]
        for h in range(1, IDX_HEADS):
            acc = acc + s[:, hcols(h)]
        valid = (j * QB + s_rel) <= (i * QB + q_rel)
        keys_ref[pl.ds(off, QB), :] = jnp.where(valid, _to_key(acc), INT_MIN)
        return carry

    lax.fori_loop(0, nkb, scores, 0)

    def count(pred):
        def body(j, acc):
            off = pl.multiple_of(j * QB, QB)
            kt = keys_ref[pl.ds(off, QB), :]
            c = jnp.where(pred(kt, j * QB + s_rel), 1, 0).astype(I32)
            return acc + jnp.sum(c.reshape(QB // SUBLANES, SUBLANES, QB), axis=0)
        acc = lax.fori_loop(0, nkb, body, jnp.zeros((SUBLANES, QB), I32))
        return jnp.sum(acc, axis=0, keepdims=True)

    tau, pc = _topk_search(count, ktop, (1, QB), nbits)

    def fill_mask(j, carry):
        off = pl.multiple_of(j * QB, QB)
        kt = keys_ref[pl.ds(off, QB), :]
        sel = _selected(kt, j * QB + s_rel, tau, pc)
        madd_ref[pl.ds(off, QB), :] = jnp.where(sel, 0.0, NEG_BIG)
        return carry

    lax.fori_loop(0, nkb, fill_mask, 0)

    scale = DSA_HD ** -0.5
    m_ref[...] = jnp.full(m_ref.shape, NEG_BIG, F32)
    l_ref[...] = jnp.zeros(l_ref.shape, F32)
    acc_ref[...] = jnp.zeros(acc_ref.shape, F32)

    def attend(j, carry):
        off = pl.multiple_of(j * QB, QB)
        madd = madd_ref[pl.ds(off, QB), :]
        madd = jnp.concatenate([madd] * G, axis=1)
        near = jnp.minimum(i - j, 2)
        for g in range(DSA_KV_HEADS):
            kj = k_ref[0, pl.ds(off, QB), g * DSA_HD:(g + 1) * DSA_HD]
            lg = _dot(kj, qt_ref[g]) * scale + bias_ref[near, g] + madd
            m = m_ref[g]
            mn = jnp.maximum(m, jnp.max(lg, axis=0, keepdims=True))
            alpha = jnp.exp(m - mn)
            p = jnp.exp(lg - mn)
            l_ref[g] = alpha * l_ref[g] + jnp.sum(p, axis=0, keepdims=True)
            m_ref[g] = mn
            acc_ref[g] = alpha * acc_ref[g] + _dot(vt_ref[0, g, j], p.astype(BF16))
        return carry

    lax.fori_loop(0, nkb, attend, 0)
    for h in range(DSA_HEADS):
        g, hh = divmod(h, G)
        out = acc_ref[g, :, hcols(hh)] / l_ref[g, :, hcols(hh)]
        o_ref[0, :, h * DSA_HD:(h + 1) * DSA_HD] = out.T.astype(o_ref.dtype)


def _dsa_prompt(rel_bias, dq, iq, sm, ik_bf, k_bf, vt_bf, t_real):
    bsz, tp = dq.shape[:2]
    nb = tp // QB
    ktop = min(TOPK_MAX, t_real // 4)
    nbits = int(tp).bit_length()
    gw = DSA_HEADS // DSA_KV_HEADS * QB
    s_rel = np.arange(QB)[:, None]
    q_rel = np.arange(QB)[None, :]
    bkt = np.stack([_rel_bucket_np(q_rel - s_rel), _rel_bucket_np(QB + q_rel - s_rel)])
    return pl.pallas_call(
        functools.partial(_dsa_prompt_kernel, ktop=ktop, nbits=nbits),
        grid=(bsz, nb),
        in_specs=[pl.BlockSpec(memory_space=pltpu.SMEM),
                  pl.BlockSpec((2, QB, QB), lambda b, i: (0, 0, 0)),
                  pl.BlockSpec((1, QB, DSA_QW), lambda b, i: (b, i, 0)),
                  pl.BlockSpec((1, QB, IDX_QW), lambda b, i: (b, i, 0)),
                  pl.BlockSpec((1, QB, LANES), lambda b, i: (b, i, 0)),
                  pl.BlockSpec((1, tp, IDX_DIM), lambda b, i: (b, 0, 0)),
                  pl.BlockSpec((1, tp, DSA_KVW), lambda b, i: (b, 0, 0)),
                  pl.BlockSpec((1, DSA_KV_HEADS, nb, DSA_HD, QB), lambda b, i: (b, 0, 0, 0, 0))],
        out_specs=pl.BlockSpec((1, QB, DSA_QW), lambda b, i: (b, i, 0)),
        out_shape=jax.ShapeDtypeStruct((bsz, tp, DSA_QW), F32),
        scratch_shapes=[pltpu.VMEM((tp, QB), I32),
                        pltpu.VMEM((tp, QB), F32),
                        pltpu.VMEM((3, DSA_KV_HEADS, QB, gw), F32),
                        pltpu.VMEM((DSA_KV_HEADS, DSA_HD, gw), BF16),
                        pltpu.VMEM((IDX_DIM, IDX_HEADS * QB), BF16),
                        pltpu.VMEM((DSA_KV_HEADS, 1, gw), F32),
                        pltpu.VMEM((DSA_KV_HEADS, 1, gw), F32),
                        pltpu.VMEM((DSA_KV_HEADS, DSA_HD, gw), F32)],
        compiler_params=_params(("arbitrary", "arbitrary")),
        name="dsa_prompt",
    )(rel_bias, jnp.asarray(bkt), dq, iq, sm, ik_bf, k_bf, vt_bf)


PP = 8


def _dsa_sample_kernel(pt_ref, rel_ref, bkt_ref, dq_ref, iq_ref, sm_ref, dk_ref, dv_ref, *rest,
                       nq, ns, ktop, nbits, past_len):
    idx_refs = rest[0:PP]
    kp_refs = rest[PP:2 * PP]
    vp_refs = rest[2 * PP:3 * PP]
    o_ref = rest[3 * PP]
    keys_ref, lg_ref, p_ref, acc_ref, linv_ref, qi_ref, qg_ref, wcol_ref = rest[3 * PP + 1:]
    step = pl.program_id(1)
    G = DSA_HEADS // DSA_KV_HEADS
    GR = G * nq
    HR = DSA_HEADS * nq
    SW = PP * PAGE_SIZE
    scale = DSA_HD ** -0.5

    @pl.when(step == 0)
    def _():
        iq = iq_ref[...]
        dq = dq_ref[...]
        sm = sm_ref[...]
        qi_ref[...] = jnp.concatenate(
            [iq[:, h * IDX_DIM:(h + 1) * IDX_DIM] for h in range(IDX_HEADS)], axis=0).astype(BF16)
        qg_ref[...] = jnp.concatenate(
            [dq[:, h * DSA_HD:(h + 1) * DSA_HD] for h in range(DSA_HEADS)], axis=0).astype(BF16)
        wcol = jnp.concatenate(
            [sm[:, SM_IW + h:SM_IW + h + 1] for h in range(IDX_HEADS)], axis=0) * (IDX_HEADS ** -0.5)
        wcol_ref[...] = jnp.broadcast_to(wcol, (HR, LANES))

    def idx_scores(kib):
        s = _dot_nt(qi_ref[...], kib)
        s = jnp.maximum(s * (IDX_DIM ** -0.5), 0.0) * wcol_ref[...]
        return jnp.sum(s.reshape(IDX_HEADS, nq, s.shape[-1]), axis=0)

    def logits(kgs):
        return jnp.concatenate(
            [_dot_nt(qg_ref[g * GR:(g + 1) * GR, :], kgs[g].astype(BF16))
             for g in range(DSA_KV_HEADS)], axis=0) * scale

    page_rows = lambda ref, g: ref[0, pl.ds(g, PAGE_SIZE, stride=DSA_KV_HEADS), :]

    @pl.when(step < ns)
    def _():
        for pi in range(PP):
            lo, hi = pi * PAGE_SIZE, (pi + 1) * PAGE_SIZE
            keys_ref[step, :, lo:hi] = _to_key(idx_scores(idx_refs[pi][0].astype(BF16)))
            lg_ref[step, :, lo:hi] = logits([page_rows(kp_refs[pi], g) for g in range(DSA_KV_HEADS)])

    @pl.when(step == ns - 1)
    def _():
        zpad = lambda z: jnp.concatenate(
            [z, jnp.zeros((PAGE_SIZE - nq, z.shape[1]), z.dtype)], axis=0)
        sm = sm_ref[...]
        n_idx = lax.broadcasted_iota(I32, (nq, PAGE_SIZE), 1)
        q_idx = lax.broadcasted_iota(I32, (nq, PAGE_SIZE), 0)
        sc_new = idx_scores(zpad(sm[:, SM_IK:SM_IK + IDX_DIM]).astype(BF16))
        keys_ref[ns, :, 0:PAGE_SIZE] = jnp.where(n_idx <= q_idx, _to_key(sc_new), INT_MIN)
        dkp = zpad(dk_ref[...])
        lg_ref[ns, :, 0:PAGE_SIZE] = logits([dkp[:, g * DSA_HD:(g + 1) * DSA_HD]
                                             for g in range(DSA_KV_HEADS)])

        lane = lax.broadcasted_iota(I32, (nq, SW), 1)

        def count(pred):
            acc = jnp.zeros((nq, PAGE_SIZE), I32)
            for s in range(ns):
                c = jnp.where(pred(keys_ref[s], s * SW + lane), 1, 0).astype(I32)
                for t in range(PP):
                    acc = acc + c[:, t * PAGE_SIZE:(t + 1) * PAGE_SIZE]
            acc = acc + jnp.where(pred(keys_ref[ns, :, 0:PAGE_SIZE], past_len + n_idx), 1, 0).astype(I32)
            return jnp.sum(acc, axis=1, keepdims=True)

        tau, pc = _topk_search(count, ktop, (nq, 1), nbits)

        far = jnp.concatenate([jnp.full((nq, 1), rel_ref[BUCKET_FAR, h], F32)
                               for h in range(DSA_HEADS)], axis=0)
        near = [jnp.concatenate([_bias_from_buckets(bkt_ref[t], rel_ref, h)
                                 for h in range(DSA_HEADS)], axis=0) for t in range(2)]
        tile8 = lambda z: jnp.concatenate([z] * DSA_HEADS, axis=0)

        m = jnp.full((HR, 1), NEG_BIG, F32)
        for s in range(ns + 1):
            if s < ns:
                sel = _selected(keys_ref[s], s * SW + lane, tau, pc)
                bias = far
                lg = lg_ref[s] + tile8(jnp.where(sel, 0.0, NEG_BIG))
                if s == ns - 1:
                    lg = jnp.concatenate(
                        [lg[:, :SW - PAGE_SIZE] + far, lg[:, SW - PAGE_SIZE:] + near[0]], axis=1)
                else:
                    lg = lg + bias
                lg_ref[s] = lg
            else:
                sel = _selected(keys_ref[ns, :, 0:PAGE_SIZE], past_len + n_idx, tau, pc)
                lg = lg_ref[ns, :, 0:PAGE_SIZE] + tile8(jnp.where(sel, 0.0, NEG_BIG)) + near[1]
                lg_ref[ns, :, 0:PAGE_SIZE] = lg
            m = jnp.maximum(m, jnp.max(lg, axis=1, keepdims=True))

        l = jnp.zeros((HR, 1), F32)
        for s in range(ns):
            p = jnp.exp(lg_ref[s] - m)
            l = l + jnp.sum(p, axis=1, keepdims=True)
            p_ref[s] = p.astype(BF16)
        pn = jnp.exp(lg_ref[ns, :, 0:PAGE_SIZE] - m)
        l = l + jnp.sum(pn, axis=1, keepdims=True)
        linv_ref[...] = jnp.broadcast_to(1.0 / l, (HR, LANES))
        vn = zpad(dv_ref[...]).astype(BF16)
        pnb = pn.astype(BF16)
        for g in range(DSA_KV_HEADS):
            acc_ref[g * GR:(g + 1) * GR, :] = _dot(pnb[g * GR:(g + 1) * GR, :],
                                                   vn[:, g * DSA_HD:(g + 1) * DSA_HD])

    @pl.when(step >= ns)
    def _():
        s = step - ns
        for pi in range(PP):
            for g in range(DSA_KV_HEADS):
                acc_ref[g * GR:(g + 1) * GR, :] += _dot(
                    p_ref[s, g * GR:(g + 1) * GR, pi * PAGE_SIZE:(pi + 1) * PAGE_SIZE],
                    page_rows(vp_refs[pi], g).astype(BF16))

    @pl.when(step == 2 * ns - 1)
    def _():
        out = acc_ref[...] * linv_ref[...]
        for h in range(DSA_HEADS):
            o_ref[:, h * DSA_HD:(h + 1) * DSA_HD] = out[h * nq:(h + 1) * nq, :]


def _dsa_sample(page_table, rel_bias, dq, iq, sm, dk, dv, cache_idx, cache_k, cache_v, layer, n_pool):
    db, n_pages = page_table.shape
    nq = dq.shape[0] // db
    assert nq == SUBLANES and n_pages % PP == 0 and nq <= PAGE_SIZE
    ns = n_pages // PP
    past_len = n_pages * PAGE_SIZE
    ktop = min(TOPK_MAX, (past_len + nq) // 4)
    nbits = int(past_len + nq).bit_length()
    base = layer * n_pool
    q_idx = np.arange(nq)[:, None]
    lane = np.arange(PAGE_SIZE)[None, :]
    bkt = np.stack([_rel_bucket_np(PAGE_SIZE + q_idx - lane), _rel_bucket_np(q_idx - lane)])
    hr = DSA_HEADS * nq

    def page_spec(rows, width, phase, pi):
        if phase == 0:
            fn = lambda b, s, pt: (base + pt[b, jnp.minimum(s, ns - 1) * PP + pi], 0, 0)
        else:
            fn = lambda b, s, pt: (base + pt[b, jnp.maximum(s - ns, 0) * PP + pi], 0, 0)
        return pl.BlockSpec((1, rows, width), fn)

    tok = lambda wd: pl.BlockSpec((nq, wd), lambda b, s, pt: (b, 0))
    grid_spec = pltpu.PrefetchScalarGridSpec(
        num_scalar_prefetch=1,
        grid=(db, 2 * ns),
        in_specs=([pl.BlockSpec(memory_space=pltpu.SMEM),
                   pl.BlockSpec((2, nq, PAGE_SIZE), lambda b, s, pt: (0, 0, 0)),
                   tok(DSA_QW), tok(IDX_QW), tok(LANES), tok(DSA_KVW), tok(DSA_KVW)]
                  + [page_spec(PAGE_SIZE, IDX_DIM, 0, pi) for pi in range(PP)]
                  + [page_spec(PAGE_SIZE * DSA_KV_HEADS, DSA_HD, 0, pi) for pi in range(PP)]
                  + [page_spec(PAGE_SIZE * DSA_KV_HEADS, DSA_HD, 1, pi) for pi in range(PP)]),
        out_specs=tok(DSA_QW),
        scratch_shapes=[pltpu.VMEM((ns + 1, nq, PP * PAGE_SIZE), I32),
                        pltpu.VMEM((ns + 1, hr, PP * PAGE_SIZE), F32),
                        pltpu.VMEM((ns, hr, PP * PAGE_SIZE), BF16),
                        pltpu.VMEM((hr, DSA_HD), F32),
                        pltpu.VMEM((hr, LANES), F32),
                        pltpu.VMEM((hr, IDX_DIM), BF16),
                        pltpu.VMEM((hr, DSA_HD), BF16),
                        pltpu.VMEM((hr, LANES), F32)])
    return pl.pallas_call(
        functools.partial(_dsa_sample_kernel, nq=nq, ns=ns, ktop=ktop, nbits=nbits, past_len=past_len),
        grid_spec=grid_spec,
        out_shape=jax.ShapeDtypeStruct((db * nq, DSA_QW), F32),
        compiler_params=_params(("arbitrary", "arbitrary")),
        name="dsa_sample",
    )(page_table, rel_bias, jnp.asarray(bkt), dq, iq, sm, dk, dv,
      *([cache_idx] * PP), *([cache_k] * PP), *([cache_v] * PP))


def _merge_kernel(x_ref, oa_ref, ob_ref, ga_ref, gb_ref, wpa_ref, wpb_ref, wout_ref, gpost_ref, o_ref):
    pa = _dot(oa_ref[...].astype(BF16), wpa_ref[...])
    pb = _dot(ob_ref[...].astype(BF16), wpb_ref[...])
    m = _sigmoid(ga_ref[...]) * pa + _sigmoid(gb_ref[...]) * pb
    mo = _dot(m.astype(BF16), wout_ref[...])
    o_ref[...] = x_ref[...] + _rms(mo, gpost_ref[...])


def _merge(x, oa, ob, ga, gb, wpa, wpb, wout, gpost, tm):
    n = x.shape[0]
    assert n % tm == 0
    row = pl.BlockSpec((tm, D_MODEL), lambda i: (i, 0))
    wsp = pl.BlockSpec((D_MODEL, D_MODEL), lambda i: (0, 0))
    return pl.pallas_call(
        _merge_kernel,
        grid=(n // tm,),
        in_specs=[row, row, row, row, row, wsp, wsp, wsp, pl.BlockSpec((1, D_MODEL), lambda i: (0, 0))],
        out_specs=row,
        out_shape=jax.ShapeDtypeStruct((n, D_MODEL), F32),
        compiler_params=_params(("arbitrary",)),
        name="merge",
    )(x, oa, ob, ga, gb, wpa, wpb, wout, gpost)


FFN_TF = 1024
PREV_ROWS = 16


def _ffn_kernel(x_ref, xprev_ref, gpre_ref, gpost_ref, wg_ref, wu_ref, cw_ref, cb_ref, wd_ref, st_ref,
                o_ref, tail_ref, h2_ref, hp_ref, gext_ref, acc_ref,
                *, shift, nprev, tiles_per_seq, use_state, tail_off):
    i = pl.program_id(0)
    j = pl.program_id(1)
    tm = x_ref.shape[0]

    @pl.when(j == 0)
    def _():
        h2_ref[...] = _rms(x_ref[...], gpre_ref[...]).astype(BF16)
        hp_ref[...] = _rms(xprev_ref[...], gpre_ref[...]).astype(BF16)
        acc_ref[...] = jnp.zeros_like(acc_ref)

    h2 = h2_ref[...]
    g = _dot(h2, wg_ref[...])
    u = _dot(h2, wu_ref[...])
    if use_state:
        gprev = st_ref[...]
    else:
        gprev = _dot(hp_ref[...], wg_ref[...])
        gprev = jnp.where(i % tiles_per_seq == 0, 0.0, gprev)
    gext_ref[0:nprev, :] = gprev
    gext_ref[nprev:nprev + tm, :] = g
    cw = cw_ref[...]
    c = (cb_ref[...] + gext_ref[nprev - 2 * shift:nprev - 2 * shift + tm, :] * cw[0:1, :]
         + gext_ref[nprev - shift:nprev - shift + tm, :] * cw[1:2, :] + g * cw[2:3, :])
    gelu = 0.5 * c * (1.0 + jnp.tanh(math.sqrt(2.0 / math.pi) * (c + 0.044715 * (c * c * c))))
    acc_ref[...] += _dot((gelu * u).astype(BF16), wd_ref[...])
    tail_ref[...] = g[tail_off:tail_off + tail_ref.shape[0], :]

    @pl.when(j == pl.num_programs(1) - 1)
    def _():
        o_ref[...] = x_ref[...] + _rms(acc_ref[...], gpost_ref[...])


def _ffn(x, gpre, gpost, w_in_bf, conv_w, conv_b, w_down_bf, state, *, tm, shift, tiles_per_seq,
         use_state, tail_off, tail_rows):
    n = x.shape[0]
    assert n % tm == 0 and tm % PREV_ROWS == 0 and D_FF % FFN_TF == 0
    nprev = state.shape[0] if use_state else PREV_ROWS
    assert nprev >= 2 * shift and (not use_state or n == tm)
    nj = D_FF // FFN_TF
    pr = tm // PREV_ROWS
    return pl.pallas_call(
        functools.partial(_ffn_kernel, shift=shift, nprev=nprev, tiles_per_seq=tiles_per_seq,
                          use_state=use_state, tail_off=tail_off),
        grid=(n // tm, nj),
        in_specs=[pl.BlockSpec((tm, D_MODEL), lambda i, j: (i, 0)),
                  pl.BlockSpec((PREV_ROWS, D_MODEL), lambda i, j: (jnp.maximum(i * pr - 1, 0), 0)),
                  pl.BlockSpec((1, D_MODEL), lambda i, j: (0, 0)),
                  pl.BlockSpec((1, D_MODEL), lambda i, j: (0, 0)),
                  pl.BlockSpec((D_MODEL, FFN_TF), lambda i, j: (0, j)),
                  pl.BlockSpec((D_MODEL, FFN_TF), lambda i, j: (0, j + nj)),
                  pl.BlockSpec((CONV_W, FFN_TF), lambda i, j: (0, j)),
                  pl.BlockSpec((1, FFN_TF), lambda i, j: (0, j)),
                  pl.BlockSpec((FFN_TF, D_MODEL), lambda i, j: (j, 0)),
                  pl.BlockSpec((state.shape[0], FFN_TF), lambda i, j: (0, j))],
        out_specs=[pl.BlockSpec((tm, D_MODEL), lambda i, j: (i, 0)),
                   pl.BlockSpec((tail_rows, FFN_TF), lambda i, j: (i, j))],
        out_shape=[jax.ShapeDtypeStruct((n, D_MODEL), F32),
                   jax.ShapeDtypeStruct((n // tm * tail_rows, D_FF), F32)],
        scratch_shapes=[pltpu.VMEM((tm, D_MODEL), BF16),
                        pltpu.VMEM((PREV_ROWS, D_MODEL), BF16),
                        pltpu.VMEM((nprev + tm, FFN_TF), F32),
                        pltpu.VMEM((tm, D_MODEL), F32)],
        compiler_params=_params(("arbitrary", "arbitrary")),
        name="ffn",
    )(x, x, gpre, gpost, w_in_bf, w_in_bf, conv_w, conv_b, w_down_bf, state)


def _round_up(a, m):
    return -(-a // m) * m


def kernel(x_prompt, x_sample, cache_k, cache_v, cache_idx_k, state_gla, state_conv, page_table, meta_tokens, rel_bias, norm_mix_pre, norm_mix_post, norm_ffn_pre, norm_ffn_post, w_in, w_a2, b_a, gla_norm, w_pa, w_pb, w_out, w_ffn_in, conv_w, conv_b, w_ffn_down):
    bsz, seq = x_prompt.shape[:2]
    db, nq = x_sample.shape[:2]
    depth = w_in.shape[0]
    n_pool = cache_k.shape[1]
    t_real = seq + N_META
    tp = _round_up(t_real, QB)
    nb = tp // QB
    row2 = lambda a: a.reshape(1, -1)

    w_in_r = [_relayout_w_in(w_in[l]) for l in range(depth)]
    w_pa_b, w_pb_b, w_out_b = w_pa.astype(BF16), w_pb.astype(BF16), w_out.astype(BF16)
    w_ffn_in_b, w_ffn_down_b = w_ffn_in.astype(BF16), w_ffn_down.astype(BF16)

    ck = cache_k.reshape(depth * n_pool, PAGE_SIZE * DSA_KV_HEADS, DSA_HD)
    cv = cache_v.reshape(depth * n_pool, PAGE_SIZE * DSA_KV_HEADS, DSA_HD)
    ci = cache_idx_k.reshape(depth * n_pool, PAGE_SIZE, IDX_DIM)

    xp = jnp.concatenate([jnp.broadcast_to(meta_tokens[None].astype(x_prompt.dtype), (bsz, N_META, D_MODEL)),
                          x_prompt, jnp.zeros((bsz, tp - t_real, D_MODEL), x_prompt.dtype)], axis=1)
    xp = xp.reshape(bsz * tp, D_MODEL)
    ffn_tm = tp // 4 if (tp // 4) % PREV_ROWS == 0 else tp
    tiles_per_seq = tp // ffn_tm
    tail_pos = (t_real - (CONV_W - 1)) % ffn_tm
    tail_off = tail_pos // SUBLANES * SUBLANES
    assert tail_pos - tail_off + (CONV_W - 1) <= SUBLANES
    pk, pv, pik, pgla, pconv = [], [], [], [], []
    zero_state = jnp.zeros((bsz, GLA_HEADS, GLA_DK, GLA_DV), F32)
    zero_conv = jnp.zeros((PREV_ROWS, D_FF), F32)
    for l in range(depth):
        z = dict(zip([n for n, _ in SEGS], _proj(xp, row2(norm_mix_pre[l]), w_in_r[l], 256)))
        r3 = lambda a: a.reshape(bsz, tp, a.shape[-1])
        o_a, s_fin = _gla(r3(z["gq"]), r3(z["gk"]), r3(z["gv"]), r3(z["gr"]), r3(z["sm"]),
                          w_a2[l], row2(b_a[l]), row2(gla_norm[l]), zero_state, t_real)
        dk3, dv3, sm3 = r3(z["dk"]), r3(z["dv"]), r3(z["sm"])
        vt = dv3.astype(BF16).reshape(bsz, nb, QB, DSA_KV_HEADS, DSA_HD).transpose(0, 3, 1, 4, 2)
        o_b = _dsa_prompt(rel_bias, r3(z["dq"]), r3(z["iq"]), sm3,
                          sm3[:, :, SM_IK:SM_IK + IDX_DIM].astype(BF16), dk3.astype(BF16), vt, t_real)
        xm = _merge(xp, o_a.reshape(bsz * tp, GLA_VW), o_b.reshape(bsz * tp, DSA_QW), z["ga"], z["gb"],
                    w_pa_b[l], w_pb_b[l], w_out_b[l], row2(norm_mix_post[l]), 512 if (bsz * tp) % 512 == 0 else QB)
        xp, tail = _ffn(xm, row2(norm_ffn_pre[l]), row2(norm_ffn_post[l]), w_ffn_in_b[l], conv_w[l],
                        row2(conv_b[l]), w_ffn_down_b[l], zero_conv, tm=ffn_tm, shift=1,
                        tiles_per_seq=tiles_per_seq, use_state=False, tail_off=tail_off, tail_rows=SUBLANES)
        pk.append(dk3[:, :t_real].reshape(bsz, t_real, DSA_KV_HEADS, DSA_HD))
        pv.append(dv3[:, :t_real].reshape(bsz, t_real, DSA_KV_HEADS, DSA_HD))
        pik.append(sm3[:, :t_real, SM_IK:SM_IK + IDX_DIM])
        pgla.append(s_fin)
        tail = tail.reshape(bsz, tiles_per_seq, SUBLANES, D_FF)[:, (t_real - 1) // ffn_tm]
        pconv.append(tail[:, tail_pos - tail_off:tail_pos - tail_off + CONV_W - 1])
    y_prompt = xp.reshape(bsz, tp, D_MODEL)[:, N_META:t_real]

    ns_rows = db * nq
    xs = x_sample.reshape(ns_rows, D_MODEL)
    sk, sv, sik, sgla, sconv = [], [], [], [], []
    for l in range(depth):
        z = dict(zip([n for n, _ in SEGS], _proj(xs, row2(norm_mix_pre[l]), w_in_r[l], ns_rows)))
        r3 = lambda a: a.reshape(db, nq, a.shape[-1])
        o_a, s_fin = _gla(r3(z["gq"]), r3(z["gk"]), r3(z["gv"]), r3(z["gr"]), r3(z["sm"]),
                          w_a2[l], row2(b_a[l]), row2(gla_norm[l]), state_gla[l], nq)
        o_b = _dsa_sample(page_table, rel_bias, z["dq"], z["iq"], z["sm"], z["dk"], z["dv"],
                          ci, ck, cv, l, n_pool)
        xm = _merge(xs, o_a.reshape(ns_rows, GLA_VW), o_b, z["ga"], z["gb"],
                    w_pa_b[l], w_pb_b[l], w_out_b[l], row2(norm_mix_post[l]), ns_rows)
        xm_t = xm.reshape(db, nq, D_MODEL).transpose(1, 0, 2).reshape(ns_rows, D_MODEL)
        st = state_conv[l].transpose(1, 0, 2).reshape((CONV_W - 1) * db, D_FF)
        xo_t, tail = _ffn(xm_t, row2(norm_ffn_pre[l]), row2(norm_ffn_post[l]), w_ffn_in_b[l], conv_w[l],
                          row2(conv_b[l]), w_ffn_down_b[l], st, tm=ns_rows, shift=db, tiles_per_seq=1,
                          use_state=True, tail_off=(nq - (CONV_W - 1)) * db, tail_rows=(CONV_W - 1) * db)
        xs = xo_t.reshape(nq, db, D_MODEL).transpose(1, 0, 2).reshape(ns_rows, D_MODEL)
        sk.append(z["dk"].reshape(db, nq, DSA_KV_HEADS, DSA_HD))
        sv.append(z["dv"].reshape(db, nq, DSA_KV_HEADS, DSA_HD))
        sik.append(z["sm"][:, SM_IK:SM_IK + IDX_DIM].reshape(db, nq, IDX_DIM))
        sgla.append(s_fin)
        sconv.append(tail.reshape(CONV_W - 1, db, D_FF).transpose(1, 0, 2))
    y_sample = xs.reshape(db, nq, D_MODEL)

    return (y_prompt, y_sample,
            jnp.stack(pk), jnp.stack(pv), jnp.stack(pik), jnp.stack(pgla), jnp.stack(pconv),
            jnp.stack(sk), jnp.stack(sv), jnp.stack(sik), jnp.stack(sgla), jnp.stack(sconv))
```

```python
import functools
import math

import jax
import jax.numpy as jnp
import numpy as np
from jax import lax
from jax.experimental import pallas as pl
from jax.experimental.pallas import tpu as pltpu

F32 = jnp.float32
BF16 = jnp.bfloat16
I32 = jnp.int32

D_MODEL = 1024
N_META = 16
GLA_HEADS = 4
GLA_DK = 128
GLA_DV = 256
GLA_RANK = 16
GLA_TAU = 16.0
DSA_HEADS = 8
DSA_KV_HEADS = 2
DSA_HD = 128
IDX_HEADS = 8
IDX_DIM = 64
TOPK_MAX = 256
REL_BUCKETS = 32
REL_MAX_DIST = 128
D_FF = 4096
CONV_W = 3
EPS = 1e-6
PAGE_SIZE = 128

GLA_KW = GLA_HEADS * GLA_DK
GLA_VW = GLA_HEADS * GLA_DV
DSA_QW = DSA_HEADS * DSA_HD
DSA_KVW = DSA_KV_HEADS * DSA_HD
IDX_QW = IDX_HEADS * IDX_DIM

LANES = 128
SUBLANES = 8
QB = 128
KB = 256
LOG2E = math.log2(math.e)
VMEM_LIMIT = 56 * 1024 * 1024

INT_MIN = -(2 ** 31)
NEG_BIG = -1e30

SEGS = (("gq", GLA_KW), ("gk", GLA_KW), ("gv", GLA_VW), ("gr", GLA_VW), ("dq", DSA_QW),
        ("dk", DSA_KVW), ("dv", DSA_KVW), ("iq", IDX_QW), ("ga", D_MODEL), ("gb", D_MODEL),
        ("sm", LANES))
SM_IK = 0
SM_GA1 = IDX_DIM
SM_IW = IDX_DIM + GLA_RANK
SEG_OFFS = tuple(int(v) for v in np.cumsum([0] + [w for _, w in SEGS]))
PW = SEG_OFFS[-1]


def _relayout_w_in(w):
    o = np.cumsum([0, GLA_KW, GLA_KW, GLA_VW, GLA_VW, GLA_RANK, DSA_QW, DSA_KVW, DSA_KVW,
                   IDX_QW, IDX_HEADS, IDX_DIM, D_MODEL, D_MODEL]).tolist()
    c = lambda i: w[:, o[i]:o[i + 1]]
    pad = jnp.zeros((w.shape[0], LANES - IDX_DIM - GLA_RANK - IDX_HEADS), w.dtype)
    cols = [c(0), c(1), c(2), c(3), c(5), c(6), c(7), c(8), c(11), c(12), c(10), c(4), c(9), pad]
    return jnp.concatenate(cols, axis=1).astype(BF16)


def _rms(x, g):
    return x * lax.rsqrt(jnp.mean(x * x, axis=-1, keepdims=True) + EPS) * g


def _sigmoid(x):
    return 1.0 / (1.0 + jnp.exp(-x))


def _dot(a, b):
    return jnp.dot(a, b, preferred_element_type=F32)


def _dot_nt(a, b):
    return lax.dot_general(a, b, (((1,), (1,)), ((), ())), preferred_element_type=F32)


def _params(sem):
    return pltpu.CompilerParams(dimension_semantics=sem, vmem_limit_bytes=VMEM_LIMIT)


def _proj_kernel(x_ref, g_ref, w_ref, *out_refs):
    hb = _rms(x_ref[...], g_ref[...]).astype(BF16)
    for o_ref, off in zip(out_refs, SEG_OFFS[:-1]):
        wd = o_ref.shape[-1]
        o_ref[...] = _dot(hb, w_ref[:, off:off + wd])


def _proj(x, g, w, tm):
    n = x.shape[0]
    assert n % tm == 0
    return pl.pallas_call(
        _proj_kernel,
        grid=(n // tm,),
        in_specs=[pl.BlockSpec((tm, D_MODEL), lambda i: (i, 0)),
                  pl.BlockSpec((1, D_MODEL), lambda i: (0, 0)),
                  pl.BlockSpec((D_MODEL, PW), lambda i: (0, 0))],
        out_specs=[pl.BlockSpec((tm, wd), lambda i: (i, 0)) for _, wd in SEGS],
        out_shape=[jax.ShapeDtypeStruct((n, wd), F32) for _, wd in SEGS],
        compiler_params=_params(("arbitrary",)),
        name="proj",
    )(x, g, w)


def _gla_kernel(q_ref, k_ref, v_ref, gr_ref, sm_ref, wa2_ref, ba_ref, gain_ref, s0_ref,
                o_ref, sfin_ref, s_ref, *, t_valid, rows):
    C = QB
    c = pl.program_id(2)

    @pl.when(c == 0)
    def _():
        s_ref[...] = s0_ref[0, 0]

    def padrows(z):
        if rows == C:
            return z
        return jnp.concatenate([z, jnp.zeros((C - rows, z.shape[1]), z.dtype)], axis=0)

    rowi = lax.broadcasted_iota(I32, (C, 1), 0)
    live = c * C + rowi < t_valid
    q = padrows(q_ref[0]) * (GLA_DK ** -0.5)
    k = jnp.where(live, padrows(k_ref[0]), 0.0)
    v = padrows(v_ref[0])
    ga1 = padrows(sm_ref[0])[:, SM_GA1:SM_GA1 + GLA_RANK]
    x = jnp.dot(ga1, wa2_ref[...], precision=lax.Precision.HIGHEST,
                preferred_element_type=F32) + ba_ref[...]
    la = (jnp.minimum(x, 0.0) - jnp.log1p(jnp.exp(-jnp.abs(x)))) * (1.0 / GLA_TAU)
    la = jnp.where(live, la, 0.0)
    ti = lax.broadcasted_iota(I32, (C, C), 0)
    si = lax.broadcasted_iota(I32, (C, C), 1)
    tri = (ti >= si).astype(F32)
    b = jnp.dot(tri, la, precision=lax.Precision.HIGHEST, preferred_element_type=F32)

    att = None
    m = C // 2
    while m >= SUBLANES:
        pieces = []
        for p in range(C // (2 * m)):
            r = p * 2 * m + m - 1
            pieces.append(jnp.broadcast_to(b[r:r + 1, :], (2 * m, GLA_DK)))
        bound = pieces[0] if len(pieces) == 1 else jnp.concatenate(pieces, axis=0)
        upper = ((rowi // m) % 2) == 1
        d = b - bound
        qm = jnp.where(upper, q * jnp.exp(jnp.minimum(d, 0.0)), 0.0)
        km = jnp.where(upper, 0.0, k * jnp.exp(jnp.minimum(-d, 0.0)))
        a = _dot_nt(qm.astype(BF16), km.astype(BF16))
        if 2 * m < C:
            a = jnp.where((ti // (2 * m)) == (si // (2 * m)), a, 0.0)
        att = a if att is None else att + a
        m //= 2

    tm8 = rowi % SUBLANES
    intra_d = jnp.zeros((C, GLA_DV), F32)
    for delta in range(SUBLANES):
        if delta == 0:
            kd, bd, vd = k, b, v
        else:
            kd = pltpu.roll(k, delta, 0)
            bd = pltpu.roll(b, delta, 0)
            vd = pltpu.roll(v, delta, 0)
        w = q * kd * jnp.exp(jnp.minimum(b - bd, 0.0))
        a = jnp.sum(w, axis=1, keepdims=True)
        a = jnp.where(tm8 >= delta, a, 0.0)
        intra_d = intra_d + a * vd

    s = s_ref[...]
    vb = v.astype(BF16)
    inter = _dot((q * jnp.exp(b)).astype(BF16), s.astype(BF16))
    o = inter + _dot(att.astype(BF16), vb) + intra_d

    b_last_row = b[C - 1:C, :]
    b_last_col = b.T[:, C - 1:C]
    kdec = k * jnp.exp(b_last_row - b)
    s_new = s * jnp.exp(b_last_col) + _dot(kdec.T.astype(BF16), vb)
    s_ref[...] = s_new

    on = o * lax.rsqrt(jnp.mean(o * o, axis=-1, keepdims=True) + EPS) * gain_ref[...]
    gr = padrows(gr_ref[0])
    res = on * (gr * _sigmoid(gr))
    o_ref[0] = res[:rows].astype(o_ref.dtype)

    @pl.when(c == pl.num_programs(2) - 1)
    def _():
        sfin_ref[0, 0] = s_new


def _gla(gq, gk, gv, gr, sm, wa2, ba, gain, s0, t_valid):
    bsz, tp = gq.shape[:2]
    rows = min(tp, QB)
    assert tp % rows == 0
    nc = tp // rows
    tok = lambda wd: pl.BlockSpec((1, rows, wd), lambda b, h, c: (b, c, h))
    return pl.pallas_call(
        functools.partial(_gla_kernel, t_valid=t_valid, rows=rows),
        grid=(bsz, GLA_HEADS, nc),
        in_specs=[tok(GLA_DK), tok(GLA_DK), tok(GLA_DV), tok(GLA_DV),
                  pl.BlockSpec((1, rows, LANES), lambda b, h, c: (b, c, 0)),
                  pl.BlockSpec((GLA_RANK, GLA_DK), lambda b, h, c: (0, h)),
                  pl.BlockSpec((1, GLA_DK), lambda b, h, c: (0, h)),
                  pl.BlockSpec((1, GLA_DV), lambda b, h, c: (0, 0)),
                  pl.BlockSpec((1, 1, GLA_DK, GLA_DV), lambda b, h, c: (b, h, 0, 0))],
        out_specs=[pl.BlockSpec((1, rows, GLA_DV), lambda b, h, c: (b, c, h)),
                   pl.BlockSpec((1, 1, GLA_DK, GLA_DV), lambda b, h, c: (b, h, 0, 0))],
        out_shape=[jax.ShapeDtypeStruct((bsz, tp, GLA_VW), F32),
                   jax.ShapeDtypeStruct((bsz, GLA_HEADS, GLA_DK, GLA_DV), F32)],
        scratch_shapes=[pltpu.VMEM((GLA_DK, GLA_DV), F32)],
        compiler_params=_params(("arbitrary", "arbitrary", "arbitrary")),
        name="gla",
    )(gq, gk, gv, gr, sm, wa2, ba, gain, s0)


def _rel_bucket_np(dist):
    max_exact = REL_BUCKETS // 2
    d = np.maximum(dist, 0)
    large = max_exact + (np.log(np.maximum(d, 1).astype(np.float64) / max_exact)
                         / math.log(REL_MAX_DIST / max_exact) * (REL_BUCKETS - max_exact)).astype(np.int64)
    large = np.minimum(large, REL_BUCKETS - 1)
    return np.where(d < max_exact, d, large).astype(np.int32)


BUCKET_FAR = int(_rel_bucket_np(np.array([QB]))[0])
assert (_rel_bucket_np(np.arange(QB, 4 * 8192)) == BUCKET_FAR).all()


def _to_key(x):
    bits = lax.bitcast_convert_type(x + 0.0, I32)
    return jnp.where(bits < 0, bits ^ 0x7FFFFFFF, bits)


def _bias_from_buckets(bkt, rel_ref, h):
    acc = jnp.zeros(bkt.shape, F32)
    for bb in range(REL_BUCKETS):
        acc = jnp.where(bkt == bb, rel_ref[bb, h], acc)
    return acc


UNCHECKED_BITS = 15
CHECK_EVERY = 2
assert (31 - UNCHECKED_BITS) % CHECK_EVERY == 0


def _topk_search(count, ktop, shape, nbits):
    def pending(fin):
        return jnp.max(jnp.where(fin > 0, 0.0, 1.0)) > 0.0

    n_valid = count(lambda kt, pos: kt > INT_MIN)
    cnt0 = count(lambda kt, pos: kt >= 0)
    tau = jnp.where(cnt0 >= ktop, jnp.zeros(shape, I32), jnp.full(shape, INT_MIN, I32))
    fin = ((n_valid <= ktop) | (cnt0 == ktop)).astype(I32)

    def one_bit(it, tau, fin):
        cand = tau | lax.shift_left(jnp.int32(1), 30 - it)
        cnt = count(lambda kt, pos: kt >= cand)
        return jnp.where(cnt >= ktop, cand, tau), fin | (cnt == ktop).astype(I32)

    tau, fin = lax.fori_loop(0, UNCHECKED_BITS, lambda it, st: one_bit(it, *st), (tau, fin))

    def bit_cond(st):
        it, _, fin = st
        return (it < 31) & pending(fin)

    def bit_body(st):
        it, tau, fin = st
        for u in range(CHECK_EVERY):
            tau, fin = one_bit(it + u, tau, fin)
        return it + CHECK_EVERY, tau, fin

    _, tau, fin = lax.while_loop(bit_cond, bit_body, (jnp.int32(UNCHECKED_BITS), tau, fin))

    def tie_search():
        need = ktop - count(lambda kt, pos: kt > tau)

        def pos_body(it, pc):
            cand = pc | lax.shift_left(jnp.int32(1), nbits - 1 - it)
            cnt = count(lambda kt, pos: (kt == tau) & (pos < cand))
            return jnp.where(cnt <= need, cand, pc)

        return lax.fori_loop(0, nbits, pos_body, jnp.zeros(shape, I32))

    pc = lax.cond(pending(fin), tie_search, lambda: jnp.full(shape, (1 << nbits) - 1, I32))
    return tau, pc


def _selected(kt, pos, tau, pc):
    return (kt > tau) | ((kt == tau) & (pos < pc) & (tau > INT_MIN))


def _dsa_prompt_kernel(rel_ref, bkt_ref, dq_ref, iq_ref, sm_ref, ik_ref, k_ref, vt_ref, o_ref,
                       keys_ref, bias_ref, qt_ref, qit_ref, m_ref, l_ref, acc_ref,
                       sa_ref, sb_ref, la_ref, lb_ref, *, ktop, nbits):
    bi = pl.program_id(0)
    i = pl.program_id(1)
    G = DSA_HEADS // DSA_KV_HEADS
    hcols = lambda hh: slice(hh * QB, (hh + 1) * QB)

    @pl.when((bi == 0) & (i == 0))
    def _():
        for h in range(DSA_HEADS):
            g, hh = divmod(h, G)
            for t in range(2):
                bias_ref[t, g, :, hcols(hh)] = _bias_from_buckets(bkt_ref[t], rel_ref, h) * LOG2E
            bias_ref[2, g, :, hcols(hh)] = jnp.full((QB, QB), rel_ref[BUCKET_FAR, h] * LOG2E, F32)

    dq = dq_ref[0] * (DSA_HD ** -0.5 * LOG2E)
    for h in range(DSA_HEADS):
        g, hh = divmod(h, G)
        qt_ref[g, :, hcols(hh)] = dq[:, h * DSA_HD:(h + 1) * DSA_HD].T.astype(BF16)
    iq = iq_ref[0]
    per = LANES // IDX_DIM
    for p in range(IDX_QW // LANES):
        t = iq[:, p * LANES:(p + 1) * LANES].T
        for r in range(per):
            qit_ref[:, hcols(p * per + r)] = t[r * IDX_DIM:(r + 1) * IDX_DIM].astype(BF16)
    smt = sm_ref[0].T
    wrow = jnp.concatenate([smt[SM_IW + h:SM_IW + h + 1, :] for h in range(IDX_HEADS)],
                           axis=1) * (IDX_HEADS ** -0.5 * IDX_DIM ** -0.5)
    assert math.log2(IDX_DIM ** -0.5).is_integer()

    s_rel = lax.broadcasted_iota(I32, (KB, QB), 0)
    q_rel = lax.broadcasted_iota(I32, (KB, QB), 1)
    s_rel2 = lax.broadcasted_iota(I32, (2 * KB, QB), 0)

    nkb = (i * QB + QB + KB - 1) // KB
    npair = (nkb + 1) // 2
    last_blk = vt_ref.shape[2] - 1
    data_off = lambda j: pl.multiple_of(jnp.minimum(j, last_blk) * KB, KB)

    def idx_dot(j):
        return _dot(ik_ref[0, pl.ds(data_off(j), KB), :], qit_ref[...])

    def score_block(j, src_ref, dst_ref):
        dst_ref[...] = idx_dot(j + 1)
        s = jnp.maximum(src_ref[...], 0.0) * wrow
        acc = s[:, hcols(0)]
        for h in range(1, IDX_HEADS):
            acc = acc + s[:, hcols(h)]
        valid = (j * KB + s_rel) <= (i * QB + q_rel)
        keys_ref[pl.ds(pl.multiple_of(j * KB, KB), KB), :] = jnp.where(valid, _to_key(acc), INT_MIN)

    sa_ref[...] = idx_dot(0)

    def scores(t, carry):
        score_block(2 * t, sa_ref, sb_ref)
        score_block(2 * t + 1, sb_ref, sa_ref)
        return carry

    lax.fori_loop(0, npair, scores, 0)

    def count(pred):
        def body(t, acc):
            kt = keys_ref[pl.ds(pl.multiple_of(t * (2 * KB), 2 * KB), 2 * KB), :]
            c = jnp.where(pred(kt, t * (2 * KB) + s_rel2), 1, 0).astype(I32)
            return acc + jnp.sum(c.reshape(2 * KB // SUBLANES, SUBLANES, QB), axis=0)
        acc = lax.fori_loop(0, npair, body, jnp.zeros((SUBLANES, QB), I32))
        return jnp.sum(acc, axis=0, keepdims=True)

    tau, pc = _topk_search(count, ktop, (1, QB), nbits)

    m_ref[...] = jnp.full(m_ref.shape, NEG_BIG, F32)
    l_ref[...] = jnp.zeros(l_ref.shape, F32)
    acc_ref[...] = jnp.zeros(acc_ref.shape, F32)
    sub = KB // QB

    def qk(j, dst_ref):
        for g in range(DSA_KV_HEADS):
            kj = k_ref[0, pl.ds(data_off(j), KB), g * DSA_HD:(g + 1) * DSA_HD]
            dst_ref[g] = _dot(kj, qt_ref[g])

    def attend_block(j, src_ref, dst_ref):
        qk(j + 1, dst_ref)
        sel = _selected(keys_ref[pl.ds(pl.multiple_of(j * KB, KB), KB), :], j * KB + s_rel, tau, pc)
        madd = jnp.where(sel, 0.0, NEG_BIG)
        madd = jnp.concatenate([madd] * G, axis=1)
        alphas, pvs = [], []
        for g in range(DSA_KV_HEADS):
            bias = jnp.concatenate(
                [bias_ref[jnp.clip(i - (j * sub + u), 0, 2), g] for u in range(sub)], axis=0)
            lg = src_ref[g] + bias + madd
            m = m_ref[g]
            mn = jnp.maximum(m, jnp.max(lg, axis=0, keepdims=True))
            alpha = jnp.exp2(m - mn)
            p = jnp.exp2(lg - mn)
            l_ref[g] = alpha * l_ref[g] + jnp.sum(p, axis=0, keepdims=True)
            m_ref[g] = mn
            alphas.append(alpha)
            pvs.append(_dot(vt_ref[0, g, jnp.minimum(j, last_blk)], p.astype(BF16)))
        for g in range(DSA_KV_HEADS):
            acc_ref[g] = alphas[g] * acc_ref[g] + pvs[g]

    qk(0, la_ref)

    def attend(t, carry):
        attend_block(2 * t, la_ref, lb_ref)
        attend_block(2 * t + 1, lb_ref, la_ref)
        return carry

    lax.fori_loop(0, npair, attend, 0)
    for h in range(DSA_HEADS):
        g, hh = divmod(h, G)
        out = acc_ref[g, :, hcols(hh)] / l_ref[g, :, hcols(hh)]
        o_ref[0, :, h * DSA_HD:(h + 1) * DSA_HD] = out.T.astype(o_ref.dtype)


def _dsa_prompt(rel_bias, dq, iq, sm, dk, dv, t_real):
    bsz, tp = dq.shape[:2]
    nb = tp // QB
    tk = _round_up(tp, KB)
    nkb = tk // KB
    padk = lambda a: jnp.pad(a.astype(BF16), ((0, 0), (0, tk - tp), (0, 0)))
    ik_bf = padk(sm[:, :, SM_IK:SM_IK + IDX_DIM])
    k_bf = padk(dk)
    vt_bf = padk(dv).reshape(bsz, nkb, KB, DSA_KV_HEADS, DSA_HD).transpose(0, 3, 1, 4, 2)
    ktop = min(TOPK_MAX, t_real // 4)
    nbits = int(tk).bit_length()
    gw = DSA_HEADS // DSA_KV_HEADS * QB
    s_rel = np.arange(QB)[:, None]
    q_rel = np.arange(QB)[None, :]
    bkt = np.stack([_rel_bucket_np(q_rel - s_rel), _rel_bucket_np(QB + q_rel - s_rel)])
    return pl.pallas_call(
        functools.partial(_dsa_prompt_kernel, ktop=ktop, nbits=nbits),
        grid=(bsz, nb),
        in_specs=[pl.BlockSpec(memory_space=pltpu.SMEM),
                  pl.BlockSpec((2, QB, QB), lambda b, i: (0, 0, 0)),
                  pl.BlockSpec((1, QB, DSA_QW), lambda b, i: (b, i, 0)),
                  pl.BlockSpec((1, QB, IDX_QW), lambda b, i: (b, i, 0)),
                  pl.BlockSpec((1, QB, LANES), lambda b, i: (b, i, 0)),
                  pl.BlockSpec((1, tk, IDX_DIM), lambda b, i: (b, 0, 0)),
                  pl.BlockSpec((1, tk, DSA_KVW), lambda b, i: (b, 0, 0)),
                  pl.BlockSpec((1, DSA_KV_HEADS, nkb, DSA_HD, KB), lambda b, i: (b, 0, 0, 0, 0))],
        out_specs=pl.BlockSpec((1, QB, DSA_QW), lambda b, i: (b, i, 0)),
        out_shape=jax.ShapeDtypeStruct((bsz, tp, DSA_QW), F32),
        scratch_shapes=[pltpu.VMEM((tk + KB, QB), I32),
                        pltpu.VMEM((3, DSA_KV_HEADS, QB, gw), F32),
                        pltpu.VMEM((DSA_KV_HEADS, DSA_HD, gw), BF16),
                        pltpu.VMEM((IDX_DIM, IDX_HEADS * QB), BF16),
                        pltpu.VMEM((DSA_KV_HEADS, 1, gw), F32),
                        pltpu.VMEM((DSA_KV_HEADS, 1, gw), F32),
                        pltpu.VMEM((DSA_KV_HEADS, DSA_HD, gw), F32),
                        pltpu.VMEM((KB, IDX_HEADS * QB), F32),
                        pltpu.VMEM((KB, IDX_HEADS * QB), F32),
                        pltpu.VMEM((DSA_KV_HEADS, KB, gw), F32),
                        pltpu.VMEM((DSA_KV_HEADS, KB, gw), F32)],
        compiler_params=_params(("arbitrary", "arbitrary")),
        name="dsa_prompt",
    )(rel_bias, jnp.asarray(bkt), dq, iq, sm, ik_bf, k_bf, vt_bf)


PP = 8


def _dsa_sample_kernel(pt_ref, rel_ref, bkt_ref, dq_ref, iq_ref, sm_ref, dk_ref, dv_ref, *rest,
                       nq, ns, ktop, nbits, past_len):
    idx_refs = rest[0:PP]
    kp_refs = rest[PP:2 * PP]
    vp_refs = rest[2 * PP:3 * PP]
    o_ref = rest[3 * PP]
    keys_ref, lg_ref, p_ref, acc_ref, linv_ref, qi_ref, qg_ref, wcol_ref = rest[3 * PP + 1:]
    step = pl.program_id(1)
    G = DSA_HEADS // DSA_KV_HEADS
    GR = G * nq
    HR = DSA_HEADS * nq
    SW = PP * PAGE_SIZE
    scale = DSA_HD ** -0.5

    @pl.when(step == 0)
    def _():
        iq = iq_ref[...]
        dq = dq_ref[...]
        sm = sm_ref[...]
        qi_ref[...] = jnp.concatenate(
            [iq[:, h * IDX_DIM:(h + 1) * IDX_DIM] for h in range(IDX_HEADS)], axis=0).astype(BF16)
        qg_ref[...] = jnp.concatenate(
            [dq[:, h * DSA_HD:(h + 1) * DSA_HD] for h in range(DSA_HEADS)], axis=0).astype(BF16)
        wcol = jnp.concatenate(
            [sm[:, SM_IW + h:SM_IW + h + 1] for h in range(IDX_HEADS)], axis=0) * (IDX_HEADS ** -0.5)
        wcol_ref[...] = jnp.broadcast_to(wcol, (HR, LANES))

    def idx_scores(kib):
        s = _dot_nt(qi_ref[...], kib)
        s = jnp.maximum(s * (IDX_DIM ** -0.5), 0.0) * wcol_ref[...]
        return jnp.sum(s.reshape(IDX_HEADS, nq, s.shape[-1]), axis=0)

    def logits(kgs):
        return jnp.concatenate(
            [_dot_nt(qg_ref[g * GR:(g + 1) * GR, :], kgs[g].astype(BF16))
             for g in range(DSA_KV_HEADS)], axis=0) * scale

    page_rows = lambda ref, g: ref[0, pl.ds(g, PAGE_SIZE, stride=DSA_KV_HEADS), :]

    @pl.when(step < ns)
    def _():
        for pi in range(PP):
            lo, hi = pi * PAGE_SIZE, (pi + 1) * PAGE_SIZE
            keys_ref[step, :, lo:hi] = _to_key(idx_scores(idx_refs[pi][0].astype(BF16)))
            lg_ref[step, :, lo:hi] = logits([page_rows(kp_refs[pi], g) for g in range(DSA_KV_HEADS)])

    @pl.when(step == ns - 1)
    def _():
        zpad = lambda z: jnp.concatenate(
            [z, jnp.zeros((PAGE_SIZE - nq, z.shape[1]), z.dtype)], axis=0)
        sm = sm_ref[...]
        n_idx = lax.broadcasted_iota(I32, (nq, PAGE_SIZE), 1)
        q_idx = lax.broadcasted_iota(I32, (nq, PAGE_SIZE), 0)
        sc_new = idx_scores(zpad(sm[:, SM_IK:SM_IK + IDX_DIM]).astype(BF16))
        keys_ref[ns, :, 0:PAGE_SIZE] = jnp.where(n_idx <= q_idx, _to_key(sc_new), INT_MIN)
        dkp = zpad(dk_ref[...])
        lg_ref[ns, :, 0:PAGE_SIZE] = logits([dkp[:, g * DSA_HD:(g + 1) * DSA_HD]
                                             for g in range(DSA_KV_HEADS)])

        lane = lax.broadcasted_iota(I32, (nq, SW), 1)

        def count(pred):
            acc = jnp.zeros((nq, PAGE_SIZE), I32)
            for s in range(ns):
                c = jnp.where(pred(keys_ref[s], s * SW + lane), 1, 0).astype(I32)
                for t in range(PP):
                    acc = acc + c[:, t * PAGE_SIZE:(t + 1) * PAGE_SIZE]
            acc = acc + jnp.where(pred(keys_ref[ns, :, 0:PAGE_SIZE], past_len + n_idx), 1, 0).astype(I32)
            return jnp.sum(acc, axis=1, keepdims=True)

        tau, pc = _topk_search(count, ktop, (nq, 1), nbits)

        far = jnp.concatenate([jnp.full((nq, 1), rel_ref[BUCKET_FAR, h], F32)
                               for h in range(DSA_HEADS)], axis=0)
        near = [jnp.concatenate([_bias_from_buckets(bkt_ref[t], rel_ref, h)
                                 for h in range(DSA_HEADS)], axis=0) for t in range(2)]
        tile8 = lambda z: jnp.concatenate([z] * DSA_HEADS, axis=0)

        m = jnp.full((HR, 1), NEG_BIG, F32)
        for s in range(ns + 1):
            if s < ns:
                sel = _selected(keys_ref[s], s * SW + lane, tau, pc)
                bias = far
                lg = lg_ref[s] + tile8(jnp.where(sel, 0.0, NEG_BIG))
                if s == ns - 1:
                    lg = jnp.concatenate(
                        [lg[:, :SW - PAGE_SIZE] + far, lg[:, SW - PAGE_SIZE:] + near[0]], axis=1)
                else:
                    lg = lg + bias
                lg_ref[s] = lg
            else:
                sel = _selected(keys_ref[ns, :, 0:PAGE_SIZE], past_len + n_idx, tau, pc)
                lg = lg_ref[ns, :, 0:PAGE_SIZE] + tile8(jnp.where(sel, 0.0, NEG_BIG)) + near[1]
                lg_ref[ns, :, 0:PAGE_SIZE] = lg
            m = jnp.maximum(m, jnp.max(lg, axis=1, keepdims=True))

        l = jnp.zeros((HR, 1), F32)
        for s in range(ns):
            p = jnp.exp(lg_ref[s] - m)
            l = l + jnp.sum(p, axis=1, keepdims=True)
            p_ref[s] = p.astype(BF16)
        pn = jnp.exp(lg_ref[ns, :, 0:PAGE_SIZE] - m)
        l = l + jnp.sum(pn, axis=1, keepdims=True)
        linv_ref[...] = jnp.broadcast_to(1.0 / l, (HR, LANES))
        vn = zpad(dv_ref[...]).astype(BF16)
        pnb = pn.astype(BF16)
        for g in range(DSA_KV_HEADS):
            acc_ref[g * GR:(g + 1) * GR, :] = _dot(pnb[g * GR:(g + 1) * GR, :],
                                                   vn[:, g * DSA_HD:(g + 1) * DSA_HD])

    @pl.when(step >= ns)
    def _():
        s = step - ns
        for pi in range(PP):
            for g in range(DSA_KV_HEADS):
                acc_ref[g * GR:(g + 1) * GR, :] += _dot(
                    p_ref[s, g * GR:(g + 1) * GR, pi * PAGE_SIZE:(pi + 1) * PAGE_SIZE],
                    page_rows(vp_refs[pi], g).astype(BF16))

    @pl.when(step == 2 * ns - 1)
    def _():
        out = acc_ref[...] * linv_ref[...]
        for h in range(DSA_HEADS):
            o_ref[:, h * DSA_HD:(h + 1) * DSA_HD] = out[h * nq:(h + 1) * nq, :]


def _dsa_sample(page_table, rel_bias, dq, iq, sm, dk, dv, cache_idx, cache_k, cache_v, layer, n_pool):
    db, n_pages = page_table.shape
    nq = dq.shape[0] // db
    assert nq == SUBLANES and n_pages % PP == 0 and nq <= PAGE_SIZE
    ns = n_pages // PP
    past_len = n_pages * PAGE_SIZE
    ktop = min(TOPK_MAX, (past_len + nq) // 4)
    nbits = int(past_len + nq).bit_length()
    base = layer * n_pool
    q_idx = np.arange(nq)[:, None]
    lane = np.arange(PAGE_SIZE)[None, :]
    bkt = np.stack([_rel_bucket_np(PAGE_SIZE + q_idx - lane), _rel_bucket_np(q_idx - lane)])
    hr = DSA_HEADS * nq

    def page_spec(rows, width, phase, pi):
        if phase == 0:
            fn = lambda b, s, pt: (base + pt[b, jnp.minimum(s, ns - 1) * PP + pi], 0, 0)
        else:
            fn = lambda b, s, pt: (base + pt[b, jnp.maximum(s - ns, 0) * PP + pi], 0, 0)
        return pl.BlockSpec((1, rows, width), fn)

    tok = lambda wd: pl.BlockSpec((nq, wd), lambda b, s, pt: (b, 0))
    grid_spec = pltpu.PrefetchScalarGridSpec(
        num_scalar_prefetch=1,
        grid=(db, 2 * ns),
        in_specs=([pl.BlockSpec(memory_space=pltpu.SMEM),
                   pl.BlockSpec((2, nq, PAGE_SIZE), lambda b, s, pt: (0, 0, 0)),
                   tok(DSA_QW), tok(IDX_QW), tok(LANES), tok(DSA_KVW), tok(DSA_KVW)]
                  + [page_spec(PAGE_SIZE, IDX_DIM, 0, pi) for pi in range(PP)]
                  + [page_spec(PAGE_SIZE * DSA_KV_HEADS, DSA_HD, 0, pi) for pi in range(PP)]
                  + [page_spec(PAGE_SIZE * DSA_KV_HEADS, DSA_HD, 1, pi) for pi in range(PP)]),
        out_specs=tok(DSA_QW),
        scratch_shapes=[pltpu.VMEM((ns + 1, nq, PP * PAGE_SIZE), I32),
                        pltpu.VMEM((ns + 1, hr, PP * PAGE_SIZE), F32),
                        pltpu.VMEM((ns, hr, PP * PAGE_SIZE), BF16),
                        pltpu.VMEM((hr, DSA_HD), F32),
                        pltpu.VMEM((hr, LANES), F32),
                        pltpu.VMEM((hr, IDX_DIM), BF16),
                        pltpu.VMEM((hr, DSA_HD), BF16),
                        pltpu.VMEM((hr, LANES), F32)])
    return pl.pallas_call(
        functools.partial(_dsa_sample_kernel, nq=nq, ns=ns, ktop=ktop, nbits=nbits, past_len=past_len),
        grid_spec=grid_spec,
        out_shape=jax.ShapeDtypeStruct((db * nq, DSA_QW), F32),
        compiler_params=_params(("arbitrary", "arbitrary")),
        name="dsa_sample",
    )(page_table, rel_bias, jnp.asarray(bkt), dq, iq, sm, dk, dv,
      *([cache_idx] * PP), *([cache_k] * PP), *([cache_v] * PP))


def _merge_kernel(x_ref, oa_ref, ob_ref, ga_ref, gb_ref, wpa_ref, wpb_ref, wout_ref, gpost_ref, o_ref):
    pa = _dot(oa_ref[...].astype(BF16), wpa_ref[...])
    pb = _dot(ob_ref[...].astype(BF16), wpb_ref[...])
    m = _sigmoid(ga_ref[...]) * pa + _sigmoid(gb_ref[...]) * pb
    mo = _dot(m.astype(BF16), wout_ref[...])
    o_ref[...] = x_ref[...] + _rms(mo, gpost_ref[...])


def _merge(x, oa, ob, ga, gb, wpa, wpb, wout, gpost, tm):
    n = x.shape[0]
    assert n % tm == 0
    row = pl.BlockSpec((tm, D_MODEL), lambda i: (i, 0))
    wsp = pl.BlockSpec((D_MODEL, D_MODEL), lambda i: (0, 0))
    return pl.pallas_call(
        _merge_kernel,
        grid=(n // tm,),
        in_specs=[row, row, row, row, row, wsp, wsp, wsp, pl.BlockSpec((1, D_MODEL), lambda i: (0, 0))],
        out_specs=row,
        out_shape=jax.ShapeDtypeStruct((n, D_MODEL), F32),
        compiler_params=_params(("arbitrary",)),
        name="merge",
    )(x, oa, ob, ga, gb, wpa, wpb, wout, gpost)


FFN_TF = 1024
PREV_ROWS = 16


def _ffn_kernel(x_ref, xprev_ref, gpre_ref, gpost_ref, wg_ref, wu_ref, cw_ref, cb_ref, wd_ref, st_ref,
                o_ref, tail_ref, h2_ref, hp_ref, gext_ref, acc_ref,
                *, shift, nprev, tiles_per_seq, use_state, tail_off):
    i = pl.program_id(0)
    j = pl.program_id(1)
    tm = x_ref.shape[0]

    @pl.when(j == 0)
    def _():
        h2_ref[...] = _rms(x_ref[...], gpre_ref[...]).astype(BF16)
        hp_ref[...] = _rms(xprev_ref[...], gpre_ref[...]).astype(BF16)
        acc_ref[...] = jnp.zeros_like(acc_ref)

    h2 = h2_ref[...]
    g = _dot(h2, wg_ref[...])
    u = _dot(h2, wu_ref[...])
    if use_state:
        gprev = st_ref[...]
    else:
        gprev = _dot(hp_ref[...], wg_ref[...])
        gprev = jnp.where(i % tiles_per_seq == 0, 0.0, gprev)
    gext_ref[0:nprev, :] = gprev
    gext_ref[nprev:nprev + tm, :] = g
    cw = cw_ref[...]
    c = (cb_ref[...] + gext_ref[nprev - 2 * shift:nprev - 2 * shift + tm, :] * cw[0:1, :]
         + gext_ref[nprev - shift:nprev - shift + tm, :] * cw[1:2, :] + g * cw[2:3, :])
    gelu = 0.5 * c * (1.0 + jnp.tanh(math.sqrt(2.0 / math.pi) * (c + 0.044715 * (c * c * c))))
    acc_ref[...] += _dot((gelu * u).astype(BF16), wd_ref[...])
    tail_ref[...] = g[tail_off:tail_off + tail_ref.shape[0], :]

    @pl.when(j == pl.num_programs(1) - 1)
    def _():
        o_ref[...] = x_ref[...] + _rms(acc_ref[...], gpost_ref[...])


def _ffn(x, gpre, gpost, w_in_bf, conv_w, conv_b, w_down_bf, state, *, tm, shift, tiles_per_seq,
         use_state, tail_off, tail_rows):
    n = x.shape[0]
    assert n % tm == 0 and tm % PREV_ROWS == 0 and D_FF % FFN_TF == 0
    nprev = state.shape[0] if use_state else PREV_ROWS
    assert nprev >= 2 * shift and (not use_state or n == tm)
    nj = D_FF // FFN_TF
    pr = tm // PREV_ROWS
    return pl.pallas_call(
        functools.partial(_ffn_kernel, shift=shift, nprev=nprev, tiles_per_seq=tiles_per_seq,
                          use_state=use_state, tail_off=tail_off),
        grid=(n // tm, nj),
        in_specs=[pl.BlockSpec((tm, D_MODEL), lambda i, j: (i, 0)),
                  pl.BlockSpec((PREV_ROWS, D_MODEL), lambda i, j: (jnp.maximum(i * pr - 1, 0), 0)),
                  pl.BlockSpec((1, D_MODEL), lambda i, j: (0, 0)),
                  pl.BlockSpec((1, D_MODEL), lambda i, j: (0, 0)),
                  pl.BlockSpec((D_MODEL, FFN_TF), lambda i, j: (0, j)),
                  pl.BlockSpec((D_MODEL, FFN_TF), lambda i, j: (0, j + nj)),
                  pl.BlockSpec((CONV_W, FFN_TF), lambda i, j: (0, j)),
                  pl.BlockSpec((1, FFN_TF), lambda i, j: (0, j)),
                  pl.BlockSpec((FFN_TF, D_MODEL), lambda i, j: (j, 0)),
                  pl.BlockSpec((state.shape[0], FFN_TF), lambda i, j: (0, j))],
        out_specs=[pl.BlockSpec((tm, D_MODEL), lambda i, j: (i, 0)),
                   pl.BlockSpec((tail_rows, FFN_TF), lambda i, j: (i, j))],
        out_shape=[jax.ShapeDtypeStruct((n, D_MODEL), F32),
                   jax.ShapeDtypeStruct((n // tm * tail_rows, D_FF), F32)],
        scratch_shapes=[pltpu.VMEM((tm, D_MODEL), BF16),
                        pltpu.VMEM((PREV_ROWS, D_MODEL), BF16),
                        pltpu.VMEM((nprev + tm, FFN_TF), F32),
                        pltpu.VMEM((tm, D_MODEL), F32)],
        compiler_params=_params(("arbitrary", "arbitrary")),
        name="ffn",
    )(x, x, gpre, gpost, w_in_bf, w_in_bf, conv_w, conv_b, w_down_bf, state)


def _round_up(a, m):
    return -(-a // m) * m


def kernel(x_prompt, x_sample, cache_k, cache_v, cache_idx_k, state_gla, state_conv, page_table, meta_tokens, rel_bias, norm_mix_pre, norm_mix_post, norm_ffn_pre, norm_ffn_post, w_in, w_a2, b_a, gla_norm, w_pa, w_pb, w_out, w_ffn_in, conv_w, conv_b, w_ffn_down):
    bsz, seq = x_prompt.shape[:2]
    db, nq = x_sample.shape[:2]
    depth = w_in.shape[0]
    n_pool = cache_k.shape[1]
    t_real = seq + N_META
    tp = _round_up(t_real, QB)
    nb = tp // QB
    row2 = lambda a: a.reshape(1, -1)

    w_in_r = [_relayout_w_in(w_in[l]) for l in range(depth)]
    w_pa_b, w_pb_b, w_out_b = w_pa.astype(BF16), w_pb.astype(BF16), w_out.astype(BF16)
    w_ffn_in_b, w_ffn_down_b = w_ffn_in.astype(BF16), w_ffn_down.astype(BF16)

    ck = cache_k.reshape(depth * n_pool, PAGE_SIZE * DSA_KV_HEADS, DSA_HD)
    cv = cache_v.reshape(depth * n_pool, PAGE_SIZE * DSA_KV_HEADS, DSA_HD)
    ci = cache_idx_k.reshape(depth * n_pool, PAGE_SIZE, IDX_DIM)

    xp = jnp.concatenate([jnp.broadcast_to(meta_tokens[None].astype(x_prompt.dtype), (bsz, N_META, D_MODEL)),
                          x_prompt, jnp.zeros((bsz, tp - t_real, D_MODEL), x_prompt.dtype)], axis=1)
    xp = xp.reshape(bsz * tp, D_MODEL)
    ffn_tm = tp // 4 if (tp // 4) % PREV_ROWS == 0 else tp
    tiles_per_seq = tp // ffn_tm
    tail_pos = (t_real - (CONV_W - 1)) % ffn_tm
    tail_off = tail_pos // SUBLANES * SUBLANES
    assert tail_pos - tail_off + (CONV_W - 1) <= SUBLANES
    pk, pv, pik, pgla, pconv = [], [], [], [], []
    zero_state = jnp.zeros((bsz, GLA_HEADS, GLA_DK, GLA_DV), F32)
    zero_conv = jnp.zeros((PREV_ROWS, D_FF), F32)
    for l in range(depth):
        z = dict(zip([n for n, _ in SEGS], _proj(xp, row2(norm_mix_pre[l]), w_in_r[l], 256)))
        r3 = lambda a: a.reshape(bsz, tp, a.shape[-1])
        o_a, s_fin = _gla(r3(z["gq"]), r3(z["gk"]), r3(z["gv"]), r3(z["gr"]), r3(z["sm"]),
                          w_a2[l], row2(b_a[l]), row2(gla_norm[l]), zero_state, t_real)
        dk3, dv3, sm3 = r3(z["dk"]), r3(z["dv"]), r3(z["sm"])
        o_b = _dsa_prompt(rel_bias, r3(z["dq"]), r3(z["iq"]), sm3, dk3, dv3, t_real)
        xm = _merge(xp, o_a.reshape(bsz * tp, GLA_VW), o_b.reshape(bsz * tp, DSA_QW), z["ga"], z["gb"],
                    w_pa_b[l], w_pb_b[l], w_out_b[l], row2(norm_mix_post[l]), 512 if (bsz * tp) % 512 == 0 else QB)
        xp, tail = _ffn(xm, row2(norm_ffn_pre[l]), row2(norm_ffn_post[l]), w_ffn_in_b[l], conv_w[l],
                        row2(conv_b[l]), w_ffn_down_b[l], zero_conv, tm=ffn_tm, shift=1,
                        tiles_per_seq=tiles_per_seq, use_state=False, tail_off=tail_off, tail_rows=SUBLANES)
        pk.append(dk3[:, :t_real].reshape(bsz, t_real, DSA_KV_HEADS, DSA_HD))
        pv.append(dv3[:, :t_real].reshape(bsz, t_real, DSA_KV_HEADS, DSA_HD))
        pik.append(sm3[:, :t_real, SM_IK:SM_IK + IDX_DIM])
        pgla.append(s_fin)
        tail = tail.reshape(bsz, tiles_per_seq, SUBLANES, D_FF)[:, (t_real - 1) // ffn_tm]
        pconv.append(tail[:, tail_pos - tail_off:tail_pos - tail_off + CONV_W - 1])
    y_prompt = xp.reshape(bsz, tp, D_MODEL)[:, N_META:t_real]

    ns_rows = db * nq
    xs = x_sample.reshape(ns_rows, D_MODEL)
    sk, sv, sik, sgla, sconv = [], [], [], [], []
    for l in range(depth):
        z = dict(zip([n for n, _ in SEGS], _proj(xs, row2(norm_mix_pre[l]), w_in_r[l], ns_rows)))
        r3 = lambda a: a.reshape(db, nq, a.shape[-1])
        o_a, s_fin = _gla(r3(z["gq"]), r3(z["gk"]), r3(z["gv"]), r3(z["gr"]), r3(z["sm"]),
                          w_a2[l], row2(b_a[l]), row2(gla_norm[l]), state_gla[l], nq)
        o_b = _dsa_sample(page_table, rel_bias, z["dq"], z["iq"], z["sm"], z["dk"], z["dv"],
                          ci, ck, cv, l, n_pool)
        xm = _merge(xs, o_a.reshape(ns_rows, GLA_VW), o_b, z["ga"], z["gb"],
                    w_pa_b[l], w_pb_b[l], w_out_b[l], row2(norm_mix_post[l]), ns_rows)
        xm_t = xm.reshape(db, nq, D_MODEL).transpose(1, 0, 2).reshape(ns_rows, D_MODEL)
        st = state_conv[l].transpose(1, 0, 2).reshape((CONV_W - 1) * db, D_FF)
        xo_t, tail = _ffn(xm_t, row2(norm_ffn_pre[l]), row2(norm_ffn_post[l]), w_ffn_in_b[l], conv_w[l],
                          row2(conv_b[l]), w_ffn_down_b[l], st, tm=ns_rows, shift=db, tiles_per_seq=1,
                          use_state=True, tail_off=(nq - (CONV_W - 1)) * db, tail_rows=(CONV_W - 1) * db)
        xs = xo_t.reshape(nq, db, D_MODEL).transpose(1, 0, 2).reshape(ns_rows, D_MODEL)
        sk.append(z["dk"].reshape(db, nq, DSA_KV_HEADS, DSA_HD))
        sv.append(z["dv"].reshape(db, nq, DSA_KV_HEADS, DSA_HD))
        sik.append(z["sm"][:, SM_IK:SM_IK + IDX_DIM].reshape(db, nq, IDX_DIM))
        sgla.append(s_fin)
        sconv.append(tail.reshape(CONV_W - 1, db, D_FF).transpose(1, 0, 2))
    y_sample = xs.reshape(db, nq, D_MODEL)

    return (y_prompt, y_sample,
            jnp.stack(pk), jnp.stack(pv), jnp.stack(pik), jnp.stack(pgla), jnp.stack(pconv),
            jnp.stack(sk), jnp.stack(sv), jnp.stack(sik), jnp.stack(sgla), jnp.stack(sconv))
```

```python
import functools
import math

import jax
import jax.numpy as jnp
import numpy as np
from jax import lax
from jax.experimental import pallas as pl
from jax.experimental.pallas import tpu as pltpu

F32 = jnp.float32
BF16 = jnp.bfloat16
I32 = jnp.int32

D_MODEL = 1024
N_META = 16
GLA_HEADS = 4
GLA_DK = 128
GLA_DV = 256
GLA_RANK = 16
GLA_TAU = 16.0
DSA_HEADS = 8
DSA_KV_HEADS = 2
DSA_HD = 128
IDX_HEADS = 8
IDX_DIM = 64
TOPK_MAX = 256
REL_BUCKETS = 32
REL_MAX_DIST = 128
D_FF = 4096
CONV_W = 3
EPS = 1e-6
PAGE_SIZE = 128

GLA_KW = GLA_HEADS * GLA_DK
GLA_VW = GLA_HEADS * GLA_DV
DSA_QW = DSA_HEADS * DSA_HD
DSA_KVW = DSA_KV_HEADS * DSA_HD
IDX_QW = IDX_HEADS * IDX_DIM

LANES = 128
SUBLANES = 8
QB = 128
KB = 256
LOG2E = math.log2(math.e)
VMEM_LIMIT = 56 * 1024 * 1024

INT_MIN = -(2 ** 31)
NEG_BIG = -1e30

SEGS = (("gq", GLA_KW), ("gk", GLA_KW), ("gv", GLA_VW), ("gr", GLA_VW), ("dq", DSA_QW),
        ("dk", DSA_KVW), ("dv", DSA_KVW), ("iq", IDX_QW), ("ga", D_MODEL), ("gb", D_MODEL),
        ("sm", LANES))
SM_IK = 0
SM_GA1 = IDX_DIM
SM_IW = IDX_DIM + GLA_RANK
SEG_OFFS = tuple(int(v) for v in np.cumsum([0] + [w for _, w in SEGS]))
PW = SEG_OFFS[-1]


def _relayout_w_in(w):
    o = np.cumsum([0, GLA_KW, GLA_KW, GLA_VW, GLA_VW, GLA_RANK, DSA_QW, DSA_KVW, DSA_KVW,
                   IDX_QW, IDX_HEADS, IDX_DIM, D_MODEL, D_MODEL]).tolist()
    c = lambda i: w[:, o[i]:o[i + 1]]
    pad = jnp.zeros((w.shape[0], LANES - IDX_DIM - GLA_RANK - IDX_HEADS), w.dtype)
    cols = [c(0), c(1), c(2), c(3), c(5), c(6), c(7), c(8), c(11), c(12), c(10), c(4), c(9), pad]
    return jnp.concatenate(cols, axis=1).astype(BF16)


def _rms(x, g):
    return x * lax.rsqrt(jnp.mean(x * x, axis=-1, keepdims=True) + EPS) * g


def _sigmoid(x):
    return 1.0 / (1.0 + jnp.exp(-x))


def _dot(a, b):
    return jnp.dot(a, b, preferred_element_type=F32)


def _dot_nt(a, b):
    return lax.dot_general(a, b, (((1,), (1,)), ((), ())), preferred_element_type=F32)


def _params(sem):
    return pltpu.CompilerParams(dimension_semantics=sem, vmem_limit_bytes=VMEM_LIMIT)


def _split3(z):
    hi = z.astype(BF16)
    r = z - hi.astype(F32)
    mid = r.astype(BF16)
    return hi, mid, (r - mid.astype(F32)).astype(BF16)


def _proj_kernel(x_ref, g_ref, w_ref, wa2_ref, ba_ref, *out_refs, chunk, seq_rows, t_valid):
    tm = x_ref.shape[0]
    hb = _rms(x_ref[...], g_ref[...]).astype(BF16)
    for o_ref, off in zip(out_refs[:-1], SEG_OFFS[:-1]):
        wd = o_ref.shape[-1]
        seg = _dot(hb, w_ref[:, off:off + wd])
        o_ref[...] = seg
    assert SEGS[-1][0] == "sm"
    ga1 = seg[:, SM_GA1:SM_GA1 + GLA_RANK]
    x = jnp.dot(ga1, wa2_ref[...], precision=lax.Precision.HIGHEST,
                preferred_element_type=F32) + ba_ref[...]
    la = (jnp.minimum(x, 0.0) - jnp.log1p(jnp.exp(-jnp.abs(x)))) * (1.0 / GLA_TAU)
    row = pl.program_id(0) * tm + lax.broadcasted_iota(I32, (tm, 1), 0)
    la = jnp.where(row % seq_rows < t_valid, la, 0.0)
    ti = lax.broadcasted_iota(I32, (tm, tm), 0)
    si = lax.broadcasted_iota(I32, (tm, tm), 1)
    tri = jnp.where((ti // chunk == si // chunk) & (ti >= si), 1.0, 0.0).astype(BF16)
    out_refs[-1][...] = sum(_dot(tri, piece) for piece in _split3(la))


def _proj(x, g, w, wa2, ba, tm, chunk, seq_rows, t_valid):
    n = x.shape[0]
    assert n % tm == 0 and tm % chunk == 0 and seq_rows % chunk == 0
    widths = [wd for _, wd in SEGS] + [GLA_KW]
    full = lambda a: pl.BlockSpec(a.shape, lambda i: (0, 0))
    return pl.pallas_call(
        functools.partial(_proj_kernel, chunk=chunk, seq_rows=seq_rows, t_valid=t_valid),
        grid=(n // tm,),
        in_specs=[pl.BlockSpec((tm, D_MODEL), lambda i: (i, 0)), full(g), full(w), full(wa2), full(ba)],
        out_specs=[pl.BlockSpec((tm, wd), lambda i: (i, 0)) for wd in widths],
        out_shape=[jax.ShapeDtypeStruct((n, wd), F32) for wd in widths],
        compiler_params=_params(("arbitrary",)),
        name="proj",
    )(x, g, w, wa2, ba)


def _gla_kernel(q_ref, k_ref, v_ref, gr_ref, bc_ref, gain_ref, s0_ref,
                o_ref, sfin_ref, s_ref, *, t_valid, rows):
    C = QB
    c = pl.program_id(2)

    @pl.when(c == 0)
    def _():
        s_ref[...] = s0_ref[0, 0]

    def padrows(z):
        if rows == C:
            return z
        return jnp.concatenate([z, jnp.zeros((C - rows, z.shape[1]), z.dtype)], axis=0)

    rowi = lax.broadcasted_iota(I32, (C, 1), 0)
    live = c * C + rowi < t_valid
    q = padrows(q_ref[0]) * (GLA_DK ** -0.5)
    k = jnp.where(live, padrows(k_ref[0]), 0.0)
    v = padrows(v_ref[0])
    ti = lax.broadcasted_iota(I32, (C, C), 0)
    si = lax.broadcasted_iota(I32, (C, C), 1)
    b = bc_ref[0]
    if rows < C:
        b = jnp.concatenate([b, jnp.broadcast_to(b[rows - 1:rows, :], (C - rows, GLA_DK))], axis=0)

    att = None
    m = C // 2
    while m >= SUBLANES:
        pieces = []
        for p in range(C // (2 * m)):
            r = p * 2 * m + m - 1
            pieces.append(jnp.broadcast_to(b[r:r + 1, :], (2 * m, GLA_DK)))
        bound = pieces[0] if len(pieces) == 1 else jnp.concatenate(pieces, axis=0)
        upper = ((rowi // m) % 2) == 1
        e = jnp.exp(-jnp.abs(b - bound))
        qm = jnp.where(upper, q * e, 0.0)
        km = jnp.where(upper, 0.0, k * e)
        a = _dot_nt(qm.astype(BF16), km.astype(BF16))
        if 2 * m < C:
            a = jnp.where((ti // (2 * m)) == (si // (2 * m)), a, 0.0)
        att = a if att is None else att + a
        m //= 2

    tm8 = rowi % SUBLANES
    diag = ti - si
    for delta in range(SUBLANES):
        kd = k if delta == 0 else pltpu.roll(k, delta, 0)
        bd = b if delta == 0 else pltpu.roll(b, delta, 0)
        w = q * kd * jnp.exp(jnp.minimum(b - bd, 0.0))
        a = jnp.sum(w, axis=1, keepdims=True)
        att = jnp.where((diag == delta) & (tm8 >= delta), a, att)

    s = s_ref[...]
    vb = v.astype(BF16)
    inter = _dot((q * jnp.exp(b)).astype(BF16), s.astype(BF16))
    o = inter + _dot(att.astype(BF16), vb)

    b_last_row = b[C - 1:C, :]
    b_last_col = b.T[:, C - 1:C]
    kdec = k * jnp.exp(b_last_row - b)
    s_new = s * jnp.exp(b_last_col) + _dot(kdec.T.astype(BF16), vb)
    s_ref[...] = s_new

    on = o * lax.rsqrt(jnp.mean(o * o, axis=-1, keepdims=True) + EPS) * gain_ref[...]
    gr = padrows(gr_ref[0])
    res = on * (gr * _sigmoid(gr))
    o_ref[0] = res[:rows].astype(o_ref.dtype)

    @pl.when(c == pl.num_programs(2) - 1)
    def _():
        sfin_ref[0, 0] = s_new


def _gla(gq, gk, gv, gr, bcum, gain, s0, t_valid):
    bsz, tp = gq.shape[:2]
    rows = min(tp, QB)
    assert tp % rows == 0
    nc = tp // rows
    tok = lambda wd: pl.BlockSpec((1, rows, wd), lambda b, h, c: (b, c, h))
    return pl.pallas_call(
        functools.partial(_gla_kernel, t_valid=t_valid, rows=rows),
        grid=(bsz, GLA_HEADS, nc),
        in_specs=[tok(GLA_DK), tok(GLA_DK), tok(GLA_DV), tok(GLA_DV), tok(GLA_DK),
                  pl.BlockSpec((1, GLA_DV), lambda b, h, c: (0, 0)),
                  pl.BlockSpec((1, 1, GLA_DK, GLA_DV), lambda b, h, c: (b, h, 0, 0))],
        out_specs=[pl.BlockSpec((1, rows, GLA_DV), lambda b, h, c: (b, c, h)),
                   pl.BlockSpec((1, 1, GLA_DK, GLA_DV), lambda b, h, c: (b, h, 0, 0))],
        out_shape=[jax.ShapeDtypeStruct((bsz, tp, GLA_VW), F32),
                   jax.ShapeDtypeStruct((bsz, GLA_HEADS, GLA_DK, GLA_DV), F32)],
        scratch_shapes=[pltpu.VMEM((GLA_DK, GLA_DV), F32)],
        compiler_params=_params(("arbitrary", "arbitrary", "arbitrary")),
        name="gla",
    )(gq, gk, gv, gr, bcum, gain, s0)


def _rel_bucket_np(dist):
    max_exact = REL_BUCKETS // 2
    d = np.maximum(dist, 0)
    large = max_exact + (np.log(np.maximum(d, 1).astype(np.float64) / max_exact)
                         / math.log(REL_MAX_DIST / max_exact) * (REL_BUCKETS - max_exact)).astype(np.int64)
    large = np.minimum(large, REL_BUCKETS - 1)
    return np.where(d < max_exact, d, large).astype(np.int32)


BUCKET_FAR = int(_rel_bucket_np(np.array([QB]))[0])
assert (_rel_bucket_np(np.arange(QB, 4 * 8192)) == BUCKET_FAR).all()


def _to_key(x):
    bits = lax.bitcast_convert_type(x + 0.0, I32)
    return jnp.where(bits < 0, bits ^ 0x7FFFFFFF, bits)


def _bias_from_buckets(bkt, rel_ref, h):
    acc = jnp.zeros(bkt.shape, F32)
    for bb in range(REL_BUCKETS):
        acc = jnp.where(bkt == bb, rel_ref[bb, h], acc)
    return acc


INT_MAX = 2 ** 31 - 1
_FOLD_OPS = {"sum": (jnp.sum, jnp.add, 0), "max": (jnp.max, jnp.maximum, INT_MIN),
             "min": (jnp.min, jnp.minimum, INT_MAX)}
MAX_PASSES = 4 * 32 + 4
NARROW_GAP = 1 << 23


def _key_to_f32(key):
    return lax.bitcast_convert_type(jnp.where(key < 0, key ^ 0x7FFFFFFF, key), F32)


def _topk_search(fold, ktop, shape, nbits, n_valid):
    count = lambda pred: fold(lambda kt, pos: jnp.where(pred(kt, pos), 1, 0).astype(I32), "sum")

    def pending(fin):
        return jnp.max(jnp.where(fin > 0, 0.0, 1.0)) > 0.0

    def settled(lo, hi, clo):
        gap = hi - lo
        return ((clo == ktop) | (gap == 0) | (gap == 1)).astype(I32)

    kmax = fold(lambda kt, pos: kt, "max")
    lo = fold(lambda kt, pos: jnp.where(kt == INT_MIN, INT_MAX, kt), "min")
    hi = jnp.where(kmax == INT_MAX, kmax, kmax + 1)
    clo = n_valid + jnp.zeros(shape, I32)
    chi = jnp.zeros(shape, I32)
    fin = (n_valid <= ktop).astype(I32) | settled(lo, hi, clo)

    def one_pass(bisect, lo, hi, clo, chi, fin):
        gap = hi - lo
        narrow = (gap >= 0) & (gap < NARROW_GAP)
        vlo, vhi = _key_to_f32(lo), _key_to_f32(hi)
        if bisect is None:
            frac = ((clo - ktop).astype(F32) + 0.5) / (clo - chi).astype(F32)
            cand = jnp.where(narrow, lo + (gap.astype(F32) * frac).astype(I32),
                             _to_key(vlo + (vhi - vlo) * frac))
        else:
            half = lo + lax.shift_right_logical(gap, 1)
            cand = jnp.where(narrow | bisect, half, _to_key(0.5 * vlo + 0.5 * vhi))
        cand = jnp.minimum(jnp.maximum(cand, lo + 1), hi - 1)
        cnt = count(lambda kt, pos: kt >= cand)
        up = (fin == 0) & (cnt >= ktop)
        down = (fin == 0) & (cnt < ktop)
        lo, clo = jnp.where(up, cand, lo), jnp.where(up, cnt, clo)
        hi, chi = jnp.where(down, cand, hi), jnp.where(down, cnt, chi)
        return lo, hi, clo, chi, fin | settled(lo, hi, clo)

    def cond(st):
        return (st[0] < MAX_PASSES) & pending(st[-1])

    def body(st):
        it, *br = st
        br = one_pass(None, *br)
        br = one_pass(it % 4 == 2, *br)
        return (it + 2, *br)

    _, lo, hi, clo, chi, fin = lax.while_loop(cond, body, (jnp.int32(0), lo, hi, clo, chi, fin))
    few = n_valid <= ktop
    tau = jnp.where(few, INT_MIN, lo)
    fin = jnp.where(few | (clo == ktop), 1, 0)

    def tie_search():
        need = ktop - count(lambda kt, pos: kt > tau)

        def pos_body(it, pc):
            cand = pc | lax.shift_left(jnp.int32(1), nbits - 1 - it)
            cnt = count(lambda kt, pos: (kt == tau) & (pos < cand))
            return jnp.where(cnt <= need, cand, pc)

        return lax.fori_loop(0, nbits, pos_body, jnp.zeros(shape, I32))

    pc = lax.cond(pending(fin), tie_search, lambda: jnp.full(shape, (1 << nbits) - 1, I32))
    return tau, pc


def _selected(kt, pos, tau, pc):
    return (kt > tau) | ((kt == tau) & (pos < pc) & (tau > INT_MIN))


def _dsa_prompt_kernel(rel_ref, bkt_ref, dq_ref, iq_ref, sm_ref, ik_ref, k_ref, vt_ref, o_ref,
                       keys_ref, bias_ref, qt_ref, qit_ref, m_ref, acc_ref,
                       sa_ref, sb_ref, la_ref, lb_ref, *, ktop, nbits):
    bi = pl.program_id(0)
    i = pl.program_id(1)
    G = DSA_HEADS // DSA_KV_HEADS
    hcols = lambda hh: slice(hh * QB, (hh + 1) * QB)

    @pl.when((bi == 0) & (i == 0))
    def _():
        for h in range(DSA_HEADS):
            g, hh = divmod(h, G)
            for t in range(2):
                bias_ref[t, g, :, hcols(hh)] = _bias_from_buckets(bkt_ref[t], rel_ref, h) * LOG2E
            bias_ref[2, g, :, hcols(hh)] = jnp.full((QB, QB), rel_ref[BUCKET_FAR, h] * LOG2E, F32)

    dq = dq_ref[0] * (DSA_HD ** -0.5 * LOG2E)
    for h in range(DSA_HEADS):
        g, hh = divmod(h, G)
        qt_ref[g, :, hcols(hh)] = dq[:, h * DSA_HD:(h + 1) * DSA_HD].T.astype(BF16)
    iq = iq_ref[0]
    per = LANES // IDX_DIM
    for p in range(IDX_QW // LANES):
        t = iq[:, p * LANES:(p + 1) * LANES].T
        for r in range(per):
            qit_ref[:, hcols(p * per + r)] = t[r * IDX_DIM:(r + 1) * IDX_DIM].astype(BF16)
    smt = sm_ref[0].T
    wrow = jnp.concatenate([smt[SM_IW + h:SM_IW + h + 1, :] for h in range(IDX_HEADS)],
                           axis=1) * (IDX_HEADS ** -0.5 * IDX_DIM ** -0.5)
    assert math.log2(IDX_DIM ** -0.5).is_integer()

    s_rel = lax.broadcasted_iota(I32, (KB, QB), 0)
    q_rel = lax.broadcasted_iota(I32, (KB, QB), 1)
    s_rel2 = lax.broadcasted_iota(I32, (2 * KB, QB), 0)

    nkb = (i * QB + QB + KB - 1) // KB
    npair = (nkb + 1) // 2
    last_blk = vt_ref.shape[2] - 1
    data_off = lambda j: pl.multiple_of(jnp.minimum(j, last_blk) * KB, KB)

    def idx_dot(j):
        return _dot(ik_ref[0, pl.ds(data_off(j), KB), :], qit_ref[...])

    def score_block(j, src_ref, dst_ref):
        dst_ref[...] = idx_dot(j + 1)
        s = jnp.maximum(src_ref[...], 0.0) * wrow
        acc = s[:, hcols(0)]
        for h in range(1, IDX_HEADS):
            acc = acc + s[:, hcols(h)]
        valid = (j * KB + s_rel) <= (i * QB + q_rel)
        keys_ref[pl.ds(pl.multiple_of(j * KB, KB), KB), :] = jnp.where(valid, _to_key(acc), INT_MIN)

    sa_ref[...] = idx_dot(0)

    def scores(t, carry):
        score_block(2 * t, sa_ref, sb_ref)
        score_block(2 * t + 1, sb_ref, sa_ref)
        return carry

    lax.fori_loop(0, npair, scores, 0)

    def fold(fn, op):
        red, comb, init = _FOLD_OPS[op]

        def body(t, acc):
            kt = keys_ref[pl.ds(pl.multiple_of(t * (2 * KB), 2 * KB), 2 * KB), :]
            c = fn(kt, t * (2 * KB) + s_rel2)
            return comb(acc, red(c.reshape(2 * KB // SUBLANES, SUBLANES, QB), axis=0))
        acc = lax.fori_loop(0, npair, body, jnp.full((SUBLANES, QB), init, I32))
        return red(acc, axis=0, keepdims=True)

    n_valid = i * QB + lax.broadcasted_iota(I32, (1, QB), 1) + 1
    tau, pc = _topk_search(fold, ktop, (1, QB), nbits, n_valid)

    m_ref[...] = jnp.full(m_ref.shape, NEG_BIG, F32)
    acc_ref[...] = jnp.zeros(acc_ref.shape, F32)
    sub = KB // QB

    def qk(j, dst_ref):
        for g in range(DSA_KV_HEADS):
            kj = k_ref[0, pl.ds(data_off(j), KB), g * DSA_HD:(g + 1) * DSA_HD]
            dst_ref[g] = _dot(kj, qt_ref[g])

    def attend_block(j, src_ref, dst_ref):
        qk(j + 1, dst_ref)
        sel = _selected(keys_ref[pl.ds(pl.multiple_of(j * KB, KB), KB), :], j * KB + s_rel, tau, pc)
        madd = jnp.where(sel, 0.0, NEG_BIG)
        madd = jnp.concatenate([madd] * G, axis=1)
        alphas, pvs = [], []
        for g in range(DSA_KV_HEADS):
            bias = jnp.concatenate(
                [bias_ref[jnp.clip(i - (j * sub + u), 0, 2), g] for u in range(sub)], axis=0)
            lg = src_ref[g] + bias + madd
            m = m_ref[g]
            mn = jnp.maximum(m, jnp.max(lg, axis=0, keepdims=True))
            alpha = jnp.exp2(m - mn)
            p = jnp.exp2(lg - mn)
            m_ref[g] = mn
            alphas.append(alpha)
            pvs.append(_dot(vt_ref[0, g, jnp.minimum(j, last_blk)], p.astype(BF16)))
        for g in range(DSA_KV_HEADS):
            acc_ref[g] = alphas[g] * acc_ref[g] + pvs[g]

    qk(0, la_ref)

    def attend(t, carry):
        attend_block(2 * t, la_ref, lb_ref)
        attend_block(2 * t + 1, lb_ref, la_ref)
        return carry

    lax.fori_loop(0, npair, attend, 0)
    for h in range(DSA_HEADS):
        g, hh = divmod(h, G)
        out = acc_ref[g, 0:DSA_HD, hcols(hh)] / acc_ref[g, DSA_HD:DSA_HD + 1, hcols(hh)]
        o_ref[0, :, h * DSA_HD:(h + 1) * DSA_HD] = out.T.astype(o_ref.dtype)


def _dsa_prompt(rel_bias, dq, iq, sm, dk, dv, t_real):
    bsz, tp = dq.shape[:2]
    nb = tp // QB
    tk = _round_up(tp, KB)
    nkb = tk // KB
    padk = lambda a: jnp.pad(a.astype(BF16), ((0, 0), (0, tk - tp), (0, 0)))
    ik_bf = padk(sm[:, :, SM_IK:SM_IK + IDX_DIM])
    k_bf = padk(dk)
    vt_bf = padk(dv).reshape(bsz, nkb, KB, DSA_KV_HEADS, DSA_HD).transpose(0, 3, 1, 4, 2)
    vt_bf = jnp.concatenate([vt_bf, jnp.ones(vt_bf.shape[:3] + (SUBLANES, KB), BF16)], axis=3)
    vrows = DSA_HD + SUBLANES
    ktop = min(TOPK_MAX, t_real // 4)
    nbits = int(tk).bit_length()
    gw = DSA_HEADS // DSA_KV_HEADS * QB
    s_rel = np.arange(QB)[:, None]
    q_rel = np.arange(QB)[None, :]
    bkt = np.stack([_rel_bucket_np(q_rel - s_rel), _rel_bucket_np(QB + q_rel - s_rel)])
    return pl.pallas_call(
        functools.partial(_dsa_prompt_kernel, ktop=ktop, nbits=nbits),
        grid=(bsz, nb),
        in_specs=[pl.BlockSpec(memory_space=pltpu.SMEM),
                  pl.BlockSpec((2, QB, QB), lambda b, i: (0, 0, 0)),
                  pl.BlockSpec((1, QB, DSA_QW), lambda b, i: (b, i, 0)),
                  pl.BlockSpec((1, QB, IDX_QW), lambda b, i: (b, i, 0)),
                  pl.BlockSpec((1, QB, LANES), lambda b, i: (b, i, 0)),
                  pl.BlockSpec((1, tk, IDX_DIM), lambda b, i: (b, 0, 0)),
                  pl.BlockSpec((1, tk, DSA_KVW), lambda b, i: (b, 0, 0)),
                  pl.BlockSpec((1, DSA_KV_HEADS, nkb, vrows, KB), lambda b, i: (b, 0, 0, 0, 0))],
        out_specs=pl.BlockSpec((1, QB, DSA_QW), lambda b, i: (b, i, 0)),
        out_shape=jax.ShapeDtypeStruct((bsz, tp, DSA_QW), F32),
        scratch_shapes=[pltpu.VMEM((tk + KB, QB), I32),
                        pltpu.VMEM((3, DSA_KV_HEADS, QB, gw), F32),
                        pltpu.VMEM((DSA_KV_HEADS, DSA_HD, gw), BF16),
                        pltpu.VMEM((IDX_DIM, IDX_HEADS * QB), BF16),
                        pltpu.VMEM((DSA_KV_HEADS, 1, gw), F32),
                        pltpu.VMEM((DSA_KV_HEADS, vrows, gw), F32),
                        pltpu.VMEM((KB, IDX_HEADS * QB), F32),
                        pltpu.VMEM((KB, IDX_HEADS * QB), F32),
                        pltpu.VMEM((DSA_KV_HEADS, KB, gw), F32),
                        pltpu.VMEM((DSA_KV_HEADS, KB, gw), F32)],
        compiler_params=_params(("arbitrary", "arbitrary")),
        name="dsa_prompt",
    )(rel_bias, jnp.asarray(bkt), dq, iq, sm, ik_bf, k_bf, vt_bf)


PP = 8


def _dsa_sample_kernel(pt_ref, rel_ref, bkt_ref, dq_ref, iq_ref, sm_ref, dk_ref, dv_ref, *rest,
                       nq, ns, ktop, nbits, past_len):
    idx_refs = rest[0:PP]
    kp_refs = rest[PP:2 * PP]
    vp_refs = rest[2 * PP:3 * PP]
    o_ref = rest[3 * PP]
    keys_ref, lg_ref, p_ref, acc_ref, linv_ref, qi_ref, qg_ref, wcol_ref = rest[3 * PP + 1:]
    step = pl.program_id(1)
    G = DSA_HEADS // DSA_KV_HEADS
    GR = G * nq
    HR = DSA_HEADS * nq
    SW = PP * PAGE_SIZE
    scale = DSA_HD ** -0.5

    @pl.when(step == 0)
    def _():
        iq = iq_ref[...]
        dq = dq_ref[...]
        sm = sm_ref[...]
        qi_ref[...] = jnp.concatenate(
            [iq[:, h * IDX_DIM:(h + 1) * IDX_DIM] for h in range(IDX_HEADS)], axis=0).astype(BF16)
        qg_ref[...] = jnp.concatenate(
            [dq[:, h * DSA_HD:(h + 1) * DSA_HD] for h in range(DSA_HEADS)], axis=0).astype(BF16)
        wcol = jnp.concatenate(
            [sm[:, SM_IW + h:SM_IW + h + 1] for h in range(IDX_HEADS)], axis=0) * (IDX_HEADS ** -0.5)
        wcol_ref[...] = jnp.broadcast_to(wcol, (HR, LANES))

    def idx_scores(kib):
        s = _dot_nt(qi_ref[...], kib)
        s = jnp.maximum(s * (IDX_DIM ** -0.5), 0.0) * wcol_ref[...]
        return jnp.sum(s.reshape(IDX_HEADS, nq, s.shape[-1]), axis=0)

    def logits(kgs):
        return jnp.concatenate(
            [_dot_nt(qg_ref[g * GR:(g + 1) * GR, :], kgs[g].astype(BF16))
             for g in range(DSA_KV_HEADS)], axis=0) * scale

    page_rows = lambda ref, g: ref[0, pl.ds(g, PAGE_SIZE, stride=DSA_KV_HEADS), :]

    @pl.when(step < ns)
    def _():
        for pi in range(PP):
            lo, hi = pi * PAGE_SIZE, (pi + 1) * PAGE_SIZE
            keys_ref[step, :, lo:hi] = _to_key(idx_scores(idx_refs[pi][0].astype(BF16)))
            lg_ref[step, :, lo:hi] = logits([page_rows(kp_refs[pi], g) for g in range(DSA_KV_HEADS)])

    @pl.when(step == ns - 1)
    def _():
        zpad = lambda z: jnp.concatenate(
            [z, jnp.zeros((PAGE_SIZE - nq, z.shape[1]), z.dtype)], axis=0)
        sm = sm_ref[...]
        n_idx = lax.broadcasted_iota(I32, (nq, PAGE_SIZE), 1)
        q_idx = lax.broadcasted_iota(I32, (nq, PAGE_SIZE), 0)
        sc_new = idx_scores(zpad(sm[:, SM_IK:SM_IK + IDX_DIM]).astype(BF16))
        keys_ref[ns, :, 0:PAGE_SIZE] = jnp.where(n_idx <= q_idx, _to_key(sc_new), INT_MIN)
        dkp = zpad(dk_ref[...])
        lg_ref[ns, :, 0:PAGE_SIZE] = logits([dkp[:, g * DSA_HD:(g + 1) * DSA_HD]
                                             for g in range(DSA_KV_HEADS)])

        lane = lax.broadcasted_iota(I32, (nq, SW), 1)

        def fold(fn, op):
            red, comb, init = _FOLD_OPS[op]
            acc = jnp.full((nq, PAGE_SIZE), init, I32)
            for s in range(ns):
                c = fn(keys_ref[s], s * SW + lane)
                for t in range(PP):
                    acc = comb(acc, c[:, t * PAGE_SIZE:(t + 1) * PAGE_SIZE])
            acc = comb(acc, fn(keys_ref[ns, :, 0:PAGE_SIZE], past_len + n_idx))
            return red(acc, axis=1, keepdims=True)

        n_valid = past_len + 1 + lax.broadcasted_iota(I32, (nq, 1), 0)
        tau, pc = _topk_search(fold, ktop, (nq, 1), nbits, n_valid)

        far = jnp.concatenate([jnp.full((nq, 1), rel_ref[BUCKET_FAR, h], F32)
                               for h in range(DSA_HEADS)], axis=0)
        near = [jnp.concatenate([_bias_from_buckets(bkt_ref[t], rel_ref, h)
                                 for h in range(DSA_HEADS)], axis=0) for t in range(2)]
        tile8 = lambda z: jnp.concatenate([z] * DSA_HEADS, axis=0)

        m = jnp.full((HR, 1), NEG_BIG, F32)
        for s in range(ns + 1):
            if s < ns:
                sel = _selected(keys_ref[s], s * SW + lane, tau, pc)
                bias = far
                lg = lg_ref[s] + tile8(jnp.where(sel, 0.0, NEG_BIG))
                if s == ns - 1:
                    lg = jnp.concatenate(
                        [lg[:, :SW - PAGE_SIZE] + far, lg[:, SW - PAGE_SIZE:] + near[0]], axis=1)
                else:
                    lg = lg + bias
                lg_ref[s] = lg
            else:
                sel = _selected(keys_ref[ns, :, 0:PAGE_SIZE], past_len + n_idx, tau, pc)
                lg = lg_ref[ns, :, 0:PAGE_SIZE] + tile8(jnp.where(sel, 0.0, NEG_BIG)) + near[1]
                lg_ref[ns, :, 0:PAGE_SIZE] = lg
            m = jnp.maximum(m, jnp.max(lg, axis=1, keepdims=True))

        l = jnp.zeros((HR, 1), F32)
        for s in range(ns):
            p = jnp.exp(lg_ref[s] - m)
            l = l + jnp.sum(p, axis=1, keepdims=True)
            p_ref[s] = p.astype(BF16)
        pn = jnp.exp(lg_ref[ns, :, 0:PAGE_SIZE] - m)
        l = l + jnp.sum(pn, axis=1, keepdims=True)
        linv_ref[...] = jnp.broadcast_to(1.0 / l, (HR, LANES))
        vn = zpad(dv_ref[...]).astype(BF16)
        pnb = pn.astype(BF16)
        for g in range(DSA_KV_HEADS):
            acc_ref[g * GR:(g + 1) * GR, :] = _dot(pnb[g * GR:(g + 1) * GR, :],
                                                   vn[:, g * DSA_HD:(g + 1) * DSA_HD])

    @pl.when(step >= ns)
    def _():
        s = step - ns
        for pi in range(PP):
            for g in range(DSA_KV_HEADS):
                acc_ref[g * GR:(g + 1) * GR, :] += _dot(
                    p_ref[s, g * GR:(g + 1) * GR, pi * PAGE_SIZE:(pi + 1) * PAGE_SIZE],
                    page_rows(vp_refs[pi], g).astype(BF16))

    @pl.when(step == 2 * ns - 1)
    def _():
        out = acc_ref[...] * linv_ref[...]
        for h in range(DSA_HEADS):
            o_ref[:, h * DSA_HD:(h + 1) * DSA_HD] = out[h * nq:(h + 1) * nq, :]


def _dsa_sample(page_table, rel_bias, dq, iq, sm, dk, dv, cache_idx, cache_k, cache_v, layer, n_pool):
    db, n_pages = page_table.shape
    nq = dq.shape[0] // db
    assert nq == SUBLANES and n_pages % PP == 0 and nq <= PAGE_SIZE
    ns = n_pages // PP
    past_len = n_pages * PAGE_SIZE
    ktop = min(TOPK_MAX, (past_len + nq) // 4)
    nbits = int(past_len + nq).bit_length()
    base = layer * n_pool
    q_idx = np.arange(nq)[:, None]
    lane = np.arange(PAGE_SIZE)[None, :]
    bkt = np.stack([_rel_bucket_np(PAGE_SIZE + q_idx - lane), _rel_bucket_np(q_idx - lane)])
    hr = DSA_HEADS * nq

    def page_spec(rows, width, phase, pi):
        if phase == 0:
            fn = lambda b, s, pt: (base + pt[b, jnp.minimum(s, ns - 1) * PP + pi], 0, 0)
        else:
            fn = lambda b, s, pt: (base + pt[b, jnp.maximum(s - ns, 0) * PP + pi], 0, 0)
        return pl.BlockSpec((1, rows, width), fn)

    tok = lambda wd: pl.BlockSpec((nq, wd), lambda b, s, pt: (b, 0))
    grid_spec = pltpu.PrefetchScalarGridSpec(
        num_scalar_prefetch=1,
        grid=(db, 2 * ns),
        in_specs=([pl.BlockSpec(memory_space=pltpu.SMEM),
                   pl.BlockSpec((2, nq, PAGE_SIZE), lambda b, s, pt: (0, 0, 0)),
                   tok(DSA_QW), tok(IDX_QW), tok(LANES), tok(DSA_KVW), tok(DSA_KVW)]
                  + [page_spec(PAGE_SIZE, IDX_DIM, 0, pi) for pi in range(PP)]
                  + [page_spec(PAGE_SIZE * DSA_KV_HEADS, DSA_HD, 0, pi) for pi in range(PP)]
                  + [page_spec(PAGE_SIZE * DSA_KV_HEADS, DSA_HD, 1, pi) for pi in range(PP)]),
        out_specs=tok(DSA_QW),
        scratch_shapes=[pltpu.VMEM((ns + 1, nq, PP * PAGE_SIZE), I32),
                        pltpu.VMEM((ns + 1, hr, PP * PAGE_SIZE), F32),
                        pltpu.VMEM((ns, hr, PP * PAGE_SIZE), BF16),
                        pltpu.VMEM((hr, DSA_HD), F32),
                        pltpu.VMEM((hr, LANES), F32),
                        pltpu.VMEM((hr, IDX_DIM), BF16),
                        pltpu.VMEM((hr, DSA_HD), BF16),
                        pltpu.VMEM((hr, LANES), F32)])
    return pl.pallas_call(
        functools.partial(_dsa_sample_kernel, nq=nq, ns=ns, ktop=ktop, nbits=nbits, past_len=past_len),
        grid_spec=grid_spec,
        out_shape=jax.ShapeDtypeStruct((db * nq, DSA_QW), F32),
        compiler_params=_params(("arbitrary", "arbitrary")),
        name="dsa_sample",
    )(page_table, rel_bias, jnp.asarray(bkt), dq, iq, sm, dk, dv,
      *([cache_idx] * PP), *([cache_k] * PP), *([cache_v] * PP))


def _merge_kernel(x_ref, oa_ref, ob_ref, ga_ref, gb_ref, wpa_ref, wpb_ref, wout_ref, gpost_ref, o_ref):
    pa = _dot(oa_ref[...].astype(BF16), wpa_ref[...])
    pb = _dot(ob_ref[...].astype(BF16), wpb_ref[...])
    m = _sigmoid(ga_ref[...]) * pa + _sigmoid(gb_ref[...]) * pb
    mo = _dot(m.astype(BF16), wout_ref[...])
    o_ref[...] = x_ref[...] + _rms(mo, gpost_ref[...])


def _merge(x, oa, ob, ga, gb, wpa, wpb, wout, gpost, tm):
    n = x.shape[0]
    assert n % tm == 0
    row = pl.BlockSpec((tm, D_MODEL), lambda i: (i, 0))
    wsp = pl.BlockSpec((D_MODEL, D_MODEL), lambda i: (0, 0))
    return pl.pallas_call(
        _merge_kernel,
        grid=(n // tm,),
        in_specs=[row, row, row, row, row, wsp, wsp, wsp, pl.BlockSpec((1, D_MODEL), lambda i: (0, 0))],
        out_specs=row,
        out_shape=jax.ShapeDtypeStruct((n, D_MODEL), F32),
        compiler_params=_params(("arbitrary",)),
        name="merge",
    )(x, oa, ob, ga, gb, wpa, wpb, wout, gpost)


FFN_TF = 1024
PREV_ROWS = 16


def _ffn_kernel(x_ref, xprev_ref, gpre_ref, gpost_ref, wg_ref, wu_ref, cw_ref, cb_ref, wd_ref, st_ref,
                o_ref, tail_ref, h2_ref, hp_ref, gext_ref, acc_ref,
                *, shift, nprev, tiles_per_seq, use_state, tail_off):
    i = pl.program_id(0)
    j = pl.program_id(1)
    tm = x_ref.shape[0]

    @pl.when(j == 0)
    def _():
        h2_ref[...] = _rms(x_ref[...], gpre_ref[...]).astype(BF16)
        hp_ref[...] = _rms(xprev_ref[...], gpre_ref[...]).astype(BF16)
        acc_ref[...] = jnp.zeros_like(acc_ref)

    h2 = h2_ref[...]
    g = _dot(h2, wg_ref[...])
    u = _dot(h2, wu_ref[...])
    if use_state:
        gprev = st_ref[...]
    else:
        gprev = _dot(hp_ref[...], wg_ref[...])
        gprev = jnp.where(i % tiles_per_seq == 0, 0.0, gprev)
    gext_ref[0:nprev, :] = gprev
    gext_ref[nprev:nprev + tm, :] = g
    cw = cw_ref[...]
    c = (cb_ref[...] + gext_ref[nprev - 2 * shift:nprev - 2 * shift + tm, :] * cw[0:1, :]
         + gext_ref[nprev - shift:nprev - shift + tm, :] * cw[1:2, :] + g * cw[2:3, :])
    gelu = 0.5 * c * (1.0 + jnp.tanh(math.sqrt(2.0 / math.pi) * (c + 0.044715 * (c * c * c))))
    acc_ref[...] += _dot((gelu * u).astype(BF16), wd_ref[...])
    tail_ref[...] = g[tail_off:tail_off + tail_ref.shape[0], :]

    @pl.when(j == pl.num_programs(1) - 1)
    def _():
        o_ref[...] = x_ref[...] + _rms(acc_ref[...], gpost_ref[...])


def _ffn(x, gpre, gpost, w_in_bf, conv_w, conv_b, w_down_bf, state, *, tm, shift, tiles_per_seq,
         use_state, tail_off, tail_rows):
    n = x.shape[0]
    assert n % tm == 0 and tm % PREV_ROWS == 0 and D_FF % FFN_TF == 0
    nprev = state.shape[0] if use_state else PREV_ROWS
    assert nprev >= 2 * shift and (not use_state or n == tm)
    nj = D_FF // FFN_TF
    pr = tm // PREV_ROWS
    return pl.pallas_call(
        functools.partial(_ffn_kernel, shift=shift, nprev=nprev, tiles_per_seq=tiles_per_seq,
                          use_state=use_state, tail_off=tail_off),
        grid=(n // tm, nj),
        in_specs=[pl.BlockSpec((tm, D_MODEL), lambda i, j: (i, 0)),
                  pl.BlockSpec((PREV_ROWS, D_MODEL), lambda i, j: (jnp.maximum(i * pr - 1, 0), 0)),
                  pl.BlockSpec((1, D_MODEL), lambda i, j: (0, 0)),
                  pl.BlockSpec((1, D_MODEL), lambda i, j: (0, 0)),
                  pl.BlockSpec((D_MODEL, FFN_TF), lambda i, j: (0, j)),
                  pl.BlockSpec((D_MODEL, FFN_TF), lambda i, j: (0, j + nj)),
                  pl.BlockSpec((CONV_W, FFN_TF), lambda i, j: (0, j)),
                  pl.BlockSpec((1, FFN_TF), lambda i, j: (0, j)),
                  pl.BlockSpec((FFN_TF, D_MODEL), lambda i, j: (j, 0)),
                  pl.BlockSpec((state.shape[0], FFN_TF), lambda i, j: (0, j))],
        out_specs=[pl.BlockSpec((tm, D_MODEL), lambda i, j: (i, 0)),
                   pl.BlockSpec((tail_rows, FFN_TF), lambda i, j: (i, j))],
        out_shape=[jax.ShapeDtypeStruct((n, D_MODEL), F32),
                   jax.ShapeDtypeStruct((n // tm * tail_rows, D_FF), F32)],
        scratch_shapes=[pltpu.VMEM((tm, D_MODEL), BF16),
                        pltpu.VMEM((PREV_ROWS, D_MODEL), BF16),
                        pltpu.VMEM((nprev + tm, FFN_TF), F32),
                        pltpu.VMEM((tm, D_MODEL), F32)],
        compiler_params=_params(("arbitrary", "arbitrary")),
        name="ffn",
    )(x, x, gpre, gpost, w_in_bf, w_in_bf, conv_w, conv_b, w_down_bf, state)


def _round_up(a, m):
    return -(-a // m) * m


def kernel(x_prompt, x_sample, cache_k, cache_v, cache_idx_k, state_gla, state_conv, page_table, meta_tokens, rel_bias, norm_mix_pre, norm_mix_post, norm_ffn_pre, norm_ffn_post, w_in, w_a2, b_a, gla_norm, w_pa, w_pb, w_out, w_ffn_in, conv_w, conv_b, w_ffn_down):
    bsz, seq = x_prompt.shape[:2]
    db, nq = x_sample.shape[:2]
    depth = w_in.shape[0]
    n_pool = cache_k.shape[1]
    t_real = seq + N_META
    tp = _round_up(t_real, QB)
    nb = tp // QB
    row2 = lambda a: a.reshape(1, -1)
    seg_names = [n for n, _ in SEGS] + ["bcum"]

    w_in_r = [_relayout_w_in(w_in[l]) for l in range(depth)]
    w_pa_b, w_pb_b, w_out_b = w_pa.astype(BF16), w_pb.astype(BF16), w_out.astype(BF16)
    w_ffn_in_b, w_ffn_down_b = w_ffn_in.astype(BF16), w_ffn_down.astype(BF16)

    ck = cache_k.reshape(depth * n_pool, PAGE_SIZE * DSA_KV_HEADS, DSA_HD)
    cv = cache_v.reshape(depth * n_pool, PAGE_SIZE * DSA_KV_HEADS, DSA_HD)
    ci = cache_idx_k.reshape(depth * n_pool, PAGE_SIZE, IDX_DIM)

    xp = jnp.concatenate([jnp.broadcast_to(meta_tokens[None].astype(x_prompt.dtype), (bsz, N_META, D_MODEL)),
                          x_prompt, jnp.zeros((bsz, tp - t_real, D_MODEL), x_prompt.dtype)], axis=1)
    xp = xp.reshape(bsz * tp, D_MODEL)
    ffn_tm = tp // 4 if (tp // 4) % PREV_ROWS == 0 else tp
    tiles_per_seq = tp // ffn_tm
    tail_pos = (t_real - (CONV_W - 1)) % ffn_tm
    tail_off = tail_pos // SUBLANES * SUBLANES
    assert tail_pos - tail_off + (CONV_W - 1) <= SUBLANES
    pk, pv, pik, pgla, pconv = [], [], [], [], []
    zero_state = jnp.zeros((bsz, GLA_HEADS, GLA_DK, GLA_DV), F32)
    zero_conv = jnp.zeros((PREV_ROWS, D_FF), F32)
    for l in range(depth):
        z = dict(zip(seg_names, _proj(xp, row2(norm_mix_pre[l]), w_in_r[l], w_a2[l], row2(b_a[l]),
                                      2 * QB, QB, tp, t_real)))
        r3 = lambda a: a.reshape(bsz, tp, a.shape[-1])
        o_a, s_fin = _gla(r3(z["gq"]), r3(z["gk"]), r3(z["gv"]), r3(z["gr"]), r3(z["bcum"]),
                          row2(gla_norm[l]), zero_state, t_real)
        dk3, dv3, sm3 = r3(z["dk"]), r3(z["dv"]), r3(z["sm"])
        o_b = _dsa_prompt(rel_bias, r3(z["dq"]), r3(z["iq"]), sm3, dk3, dv3, t_real)
        xm = _merge(xp, o_a.reshape(bsz * tp, GLA_VW), o_b.reshape(bsz * tp, DSA_QW), z["ga"], z["gb"],
                    w_pa_b[l], w_pb_b[l], w_out_b[l], row2(norm_mix_post[l]), 512 if (bsz * tp) % 512 == 0 else QB)
        xp, tail = _ffn(xm, row2(norm_ffn_pre[l]), row2(norm_ffn_post[l]), w_ffn_in_b[l], conv_w[l],
                        row2(conv_b[l]), w_ffn_down_b[l], zero_conv, tm=ffn_tm, shift=1,
                        tiles_per_seq=tiles_per_seq, use_state=False, tail_off=tail_off, tail_rows=SUBLANES)
        pk.append(dk3[:, :t_real].reshape(bsz, t_real, DSA_KV_HEADS, DSA_HD))
        pv.append(dv3[:, :t_real].reshape(bsz, t_real, DSA_KV_HEADS, DSA_HD))
        pik.append(sm3[:, :t_real, SM_IK:SM_IK + IDX_DIM])
        pgla.append(s_fin)
        tail = tail.reshape(bsz, tiles_per_seq, SUBLANES, D_FF)[:, (t_real - 1) // ffn_tm]
        pconv.append(tail[:, tail_pos - tail_off:tail_pos - tail_off + CONV_W - 1])
    y_prompt = xp.reshape(bsz, tp, D_MODEL)[:, N_META:t_real]

    ns_rows = db * nq
    xs = x_sample.reshape(ns_rows, D_MODEL)
    sk, sv, sik, sgla, sconv = [], [], [], [], []
    for l in range(depth):
        z = dict(zip(seg_names, _proj(xs, row2(norm_mix_pre[l]), w_in_r[l], w_a2[l], row2(b_a[l]),
                                      ns_rows, nq, nq, nq)))
        r3 = lambda a: a.reshape(db, nq, a.shape[-1])
        o_a, s_fin = _gla(r3(z["gq"]), r3(z["gk"]), r3(z["gv"]), r3(z["gr"]), r3(z["bcum"]),
                          row2(gla_norm[l]), state_gla[l], nq)
        o_b = _dsa_sample(page_table, rel_bias, z["dq"], z["iq"], z["sm"], z["dk"], z["dv"],
                          ci, ck, cv, l, n_pool)
        xm = _merge(xs, o_a.reshape(ns_rows, GLA_VW), o_b, z["ga"], z["gb"],
                    w_pa_b[l], w_pb_b[l], w_out_b[l], row2(norm_mix_post[l]), ns_rows)
        xm_t = xm.reshape(db, nq, D_MODEL).transpose(1, 0, 2).reshape(ns_rows, D_MODEL)
        st = state_conv[l].transpose(1, 0, 2).reshape((CONV_W - 1) * db, D_FF)
        xo_t, tail = _ffn(xm_t, row2(norm_ffn_pre[l]), row2(norm_ffn_post[l]), w_ffn_in_b[l], conv_w[l],
                          row2(conv_b[l]), w_ffn_down_b[l], st, tm=ns_rows, shift=db, tiles_per_seq=1,
                          use_state=True, tail_off=(nq - (CONV_W - 1)) * db, tail_rows=(CONV_W - 1) * db)
        xs = xo_t.reshape(nq, db, D_MODEL).transpose(1, 0, 2).reshape(ns_rows, D_MODEL)
        sk.append(z["dk"].reshape(db, nq, DSA_KV_HEADS, DSA_HD))
        sv.append(z["dv"].reshape(db, nq, DSA_KV_HEADS, DSA_HD))
        sik.append(z["sm"][:, SM_IK:SM_IK + IDX_DIM].reshape(db, nq, IDX_DIM))
        sgla.append(s_fin)
        sconv.append(tail.reshape(CONV_W - 1, db, D_FF).transpose(1, 0, 2))
    y_sample = xs.reshape(db, nq, D_MODEL)

    return (y_prompt, y_sample,
            jnp.stack(pk), jnp.stack(pv), jnp.stack(pik), jnp.stack(pgla), jnp.stack(pconv),
            jnp.stack(sk), jnp.stack(sv), jnp.stack(sik), jnp.stack(sgla), jnp.stack(sconv))
```

```python
import functools
import math

import jax
import jax.numpy as jnp
import numpy as np
from jax import lax
from jax.experimental import pallas as pl
from jax.experimental.pallas import tpu as pltpu

F32 = jnp.float32
BF16 = jnp.bfloat16
I32 = jnp.int32

D_MODEL = 1024
N_META = 16
GLA_HEADS = 4
GLA_DK = 128
GLA_DV = 256
GLA_RANK = 16
GLA_TAU = 16.0
DSA_HEADS = 8
DSA_KV_HEADS = 2
DSA_HD = 128
IDX_HEADS = 8
IDX_DIM = 64
TOPK_MAX = 256
REL_BUCKETS = 32
REL_MAX_DIST = 128
D_FF = 4096
CONV_W = 3
EPS = 1e-6
PAGE_SIZE = 128

GLA_KW = GLA_HEADS * GLA_DK
GLA_VW = GLA_HEADS * GLA_DV
DSA_QW = DSA_HEADS * DSA_HD
DSA_KVW = DSA_KV_HEADS * DSA_HD
IDX_QW = IDX_HEADS * IDX_DIM

LANES = 128
SUBLANES = 8
QB = 128
KB = 256
LOG2E = math.log2(math.e)
VMEM_LIMIT = 56 * 1024 * 1024

INT_MIN = -(2 ** 31)
NEG_BIG = -1e30

SEGS = (("gq", GLA_KW), ("gk", GLA_KW), ("gv", GLA_VW), ("gr", GLA_VW), ("dq", DSA_QW),
        ("dk", DSA_KVW), ("dv", DSA_KVW), ("iq", IDX_QW), ("ga", D_MODEL), ("gb", D_MODEL),
        ("sm", LANES))
SM_IK = 0
SM_GA1 = IDX_DIM
SM_IW = IDX_DIM + GLA_RANK
SEG_OFFS = tuple(int(v) for v in np.cumsum([0] + [w for _, w in SEGS]))
PW = SEG_OFFS[-1]


def _relayout_w_in(w):
    o = np.cumsum([0, GLA_KW, GLA_KW, GLA_VW, GLA_VW, GLA_RANK, DSA_QW, DSA_KVW, DSA_KVW,
                   IDX_QW, IDX_HEADS, IDX_DIM, D_MODEL, D_MODEL]).tolist()
    c = lambda i: w[:, o[i]:o[i + 1]]
    pad = jnp.zeros((w.shape[0], LANES - IDX_DIM - GLA_RANK - IDX_HEADS), w.dtype)
    cols = [c(0), c(1), c(2), c(3), c(5), c(6), c(7), c(8), c(11), c(12), c(10), c(4), c(9), pad]
    return jnp.concatenate(cols, axis=1).astype(BF16)


def _rms(x, g):
    return x * lax.rsqrt(jnp.mean(x * x, axis=-1, keepdims=True) + EPS) * g


def _sigmoid(x):
    return 1.0 / (1.0 + jnp.exp(-x))


def _dot(a, b):
    return jnp.dot(a, b, preferred_element_type=F32)


def _dot_nt(a, b):
    return lax.dot_general(a, b, (((1,), (1,)), ((), ())), preferred_element_type=F32)


def _params(sem):
    return pltpu.CompilerParams(dimension_semantics=sem, vmem_limit_bytes=VMEM_LIMIT)


def _split3(z):
    hi = z.astype(BF16)
    r = z - hi.astype(F32)
    mid = r.astype(BF16)
    return hi, mid, (r - mid.astype(F32)).astype(BF16)


def _proj_kernel(x_ref, g_ref, w_ref, wa2_ref, ba_ref, *out_refs, chunk, seq_rows, t_valid):
    tm = x_ref.shape[0]
    hb = _rms(x_ref[...], g_ref[...]).astype(BF16)
    segment = lambda s: _dot(hb, w_ref[:, SEG_OFFS[s]:SEG_OFFS[s + 1]])
    assert SEGS[-1][0] == "sm"
    seg = segment(len(SEGS) - 1)
    out_refs[len(SEGS) - 1][...] = seg
    ga1 = seg[:, SM_GA1:SM_GA1 + GLA_RANK]
    x = jnp.dot(ga1, wa2_ref[...], precision=lax.Precision.HIGHEST,
                preferred_element_type=F32) + ba_ref[...]
    la = (jnp.minimum(x, 0.0) - jnp.log1p(jnp.exp(-jnp.abs(x)))) * (1.0 / GLA_TAU)
    row = pl.program_id(0) * tm + lax.broadcasted_iota(I32, (tm, 1), 0)
    la = jnp.where(row % seq_rows < t_valid, la, 0.0)
    ti = lax.broadcasted_iota(I32, (tm, tm), 0)
    si = lax.broadcasted_iota(I32, (tm, tm), 1)
    tri = jnp.where((ti // chunk == si // chunk) & (ti >= si), 1.0, 0.0).astype(BF16)
    out_refs[-1][...] = sum(_dot(tri, piece) for piece in _split3(la))
    for s in range(len(SEGS) - 1):
        out_refs[s][...] = segment(s)


def _proj(x, g, w, wa2, ba, tm, chunk, seq_rows, t_valid):
    n = x.shape[0]
    assert n % tm == 0 and tm % chunk == 0 and seq_rows % chunk == 0
    widths = [wd for _, wd in SEGS] + [GLA_KW]
    full = lambda a: pl.BlockSpec(a.shape, lambda i: (0, 0))
    return pl.pallas_call(
        functools.partial(_proj_kernel, chunk=chunk, seq_rows=seq_rows, t_valid=t_valid),
        grid=(n // tm,),
        in_specs=[pl.BlockSpec((tm, D_MODEL), lambda i: (i, 0)), full(g), full(w), full(wa2), full(ba)],
        out_specs=[pl.BlockSpec((tm, wd), lambda i: (i, 0)) for wd in widths],
        out_shape=[jax.ShapeDtypeStruct((n, wd), F32) for wd in widths],
        compiler_params=_params(("arbitrary",)),
        name="proj",
    )(x, g, w, wa2, ba)


def _gla_kernel(q_ref, k_ref, v_ref, gr_ref, bc_ref, gain_ref, s0_ref,
                o_ref, sfin_ref, s_ref, *, t_valid, rows):
    C = QB
    c = pl.program_id(2)

    @pl.when(c == 0)
    def _():
        s_ref[...] = s0_ref[0, 0]

    def padrows(z):
        if rows == C:
            return z
        return jnp.concatenate([z, jnp.zeros((C - rows, z.shape[1]), z.dtype)], axis=0)

    rowi = lax.broadcasted_iota(I32, (C, 1), 0)
    live = c * C + rowi < t_valid
    q = padrows(q_ref[0]) * (GLA_DK ** -0.5)
    k = jnp.where(live, padrows(k_ref[0]), 0.0)
    v = padrows(v_ref[0])
    ti = lax.broadcasted_iota(I32, (C, C), 0)
    si = lax.broadcasted_iota(I32, (C, C), 1)
    b = bc_ref[0]
    if rows < C:
        b = jnp.concatenate([b, jnp.broadcast_to(b[rows - 1:rows, :], (C - rows, GLA_DK))], axis=0)

    att = None
    m = C // 2
    while m >= SUBLANES:
        pieces = []
        for p in range(C // (2 * m)):
            r = p * 2 * m + m - 1
            pieces.append(jnp.broadcast_to(b[r:r + 1, :], (2 * m, GLA_DK)))
        bound = pieces[0] if len(pieces) == 1 else jnp.concatenate(pieces, axis=0)
        upper = ((rowi // m) % 2) == 1
        e = jnp.exp(-jnp.abs(b - bound))
        qm = jnp.where(upper, q * e, 0.0)
        km = jnp.where(upper, 0.0, k * e)
        a = _dot_nt(qm.astype(BF16), km.astype(BF16))
        if 2 * m < C:
            a = jnp.where((ti // (2 * m)) == (si // (2 * m)), a, 0.0)
        att = a if att is None else att + a
        m //= 2

    tm8 = rowi % SUBLANES
    diag = ti - si
    for delta in range(SUBLANES):
        kd = k if delta == 0 else pltpu.roll(k, delta, 0)
        bd = b if delta == 0 else pltpu.roll(b, delta, 0)
        w = q * kd * jnp.exp(jnp.minimum(b - bd, 0.0))
        a = jnp.sum(w, axis=1, keepdims=True)
        att = jnp.where((diag == delta) & (tm8 >= delta), a, att)

    s = s_ref[...]
    vb = v.astype(BF16)
    inter = _dot((q * jnp.exp(b)).astype(BF16), s.astype(BF16))
    o = inter + _dot(att.astype(BF16), vb)

    b_last_row = b[C - 1:C, :]
    b_last_col = b.T[:, C - 1:C]
    kdec = k * jnp.exp(b_last_row - b)
    s_new = s * jnp.exp(b_last_col) + _dot(kdec.T.astype(BF16), vb)
    s_ref[...] = s_new

    on = o * lax.rsqrt(jnp.mean(o * o, axis=-1, keepdims=True) + EPS) * gain_ref[...]
    gr = padrows(gr_ref[0])
    res = on * (gr * _sigmoid(gr))
    o_ref[0] = res[:rows].astype(o_ref.dtype)

    @pl.when(c == pl.num_programs(2) - 1)
    def _():
        sfin_ref[0, 0] = s_new


def _gla(gq, gk, gv, gr, bcum, gain, s0, s0_first, t_valid):
    bsz, tp = gq.shape[:2]
    rows = min(tp, QB)
    assert tp % rows == 0
    nc = tp // rows
    tok = lambda wd: pl.BlockSpec((1, rows, wd), lambda b, h, c: (b, c, h))
    return pl.pallas_call(
        functools.partial(_gla_kernel, t_valid=t_valid, rows=rows),
        grid=(bsz, GLA_HEADS, nc),
        in_specs=[tok(GLA_DK), tok(GLA_DK), tok(GLA_DV), tok(GLA_DV), tok(GLA_DK),
                  pl.BlockSpec((1, GLA_DV), lambda b, h, c: (0, 0)),
                  pl.BlockSpec((1, 1, GLA_DK, GLA_DV), lambda b, h, c: (s0_first + b, h, 0, 0))],
        out_specs=[pl.BlockSpec((1, rows, GLA_DV), lambda b, h, c: (b, c, h)),
                   pl.BlockSpec((1, 1, GLA_DK, GLA_DV), lambda b, h, c: (b, h, 0, 0))],
        out_shape=[jax.ShapeDtypeStruct((bsz, tp, GLA_VW), F32),
                   jax.ShapeDtypeStruct((bsz, GLA_HEADS, GLA_DK, GLA_DV), F32)],
        scratch_shapes=[pltpu.VMEM((GLA_DK, GLA_DV), F32)],
        compiler_params=_params(("arbitrary", "arbitrary", "arbitrary")),
        name="gla",
    )(gq, gk, gv, gr, bcum, gain, s0)


def _rel_bucket_np(dist):
    max_exact = REL_BUCKETS // 2
    d = np.maximum(dist, 0)
    large = max_exact + (np.log(np.maximum(d, 1).astype(np.float64) / max_exact)
                         / math.log(REL_MAX_DIST / max_exact) * (REL_BUCKETS - max_exact)).astype(np.int64)
    large = np.minimum(large, REL_BUCKETS - 1)
    return np.where(d < max_exact, d, large).astype(np.int32)


BUCKET_FAR = int(_rel_bucket_np(np.array([QB]))[0])
assert (_rel_bucket_np(np.arange(QB, 4 * 8192)) == BUCKET_FAR).all()


def _to_key(x):
    bits = lax.bitcast_convert_type(x + 0.0, I32)
    return jnp.where(bits < 0, bits ^ 0x7FFFFFFF, bits)


def _bias_from_buckets(bkt, rel_ref, h):
    acc = jnp.zeros(bkt.shape, F32)
    for bb in range(REL_BUCKETS):
        acc = jnp.where(bkt == bb, rel_ref[bb, h], acc)
    return acc


_FOLD_OPS = {"sum": (jnp.sum, jnp.add, 0)}


def _topk_search(fold, ktop, shape, nbits, n_valid):
    count = lambda pred: fold(lambda kt, pos: jnp.where(pred(kt, pos), 1, 0).astype(I32), "sum")

    def pending(fin):
        return jnp.max(jnp.where(fin > 0, 0.0, 1.0)) > 0.0

    cnt0 = count(lambda kt, pos: kt >= 0)
    tau = jnp.where(cnt0 >= ktop, jnp.zeros(shape, I32), jnp.full(shape, INT_MIN, I32))
    cge = jnp.where(cnt0 >= ktop, cnt0, n_valid + jnp.zeros(shape, I32))

    def one_bit(it, st):
        tau, cge = st
        cand = tau | lax.shift_left(jnp.int32(1), 30 - it)
        cnt = count(lambda kt, pos: kt >= cand)
        up = cnt >= ktop
        return jnp.where(up, cand, tau), jnp.where(up, cnt, cge)

    tau, cge = lax.fori_loop(0, 31, one_bit, (tau, cge))
    few = n_valid <= ktop
    tau = jnp.where(few, INT_MIN, tau)
    fin = jnp.where(few | (cge == ktop), 1, 0)

    def tie_search():
        need = ktop - count(lambda kt, pos: kt > tau)

        def pos_body(it, pc):
            cand = pc | lax.shift_left(jnp.int32(1), nbits - 1 - it)
            cnt = count(lambda kt, pos: (kt == tau) & (pos < cand))
            return jnp.where(cnt <= need, cand, pc)

        return lax.fori_loop(0, nbits, pos_body, jnp.zeros(shape, I32))

    pc = lax.cond(pending(fin), tie_search, lambda: jnp.full(shape, (1 << nbits) - 1, I32))
    return tau, pc


def _selected(kt, pos, tau, pc):
    return (kt > tau) | ((kt == tau) & (pos < pc) & (tau > INT_MIN))


def _dsa_prompt_kernel(rel_ref, bkt_ref, dq_ref, iq_ref, sm_ref, ik_ref, k_ref, vt_ref, o_ref,
                       keys_ref, bias_ref, qt_ref, qit_ref, m_ref, acc_ref,
                       sa_ref, sb_ref, la_ref, lb_ref, *, ktop, nbits):
    bi = pl.program_id(0)
    i = pl.program_id(1)
    G = DSA_HEADS // DSA_KV_HEADS
    hcols = lambda hh: slice(hh * QB, (hh + 1) * QB)

    @pl.when((bi == 0) & (i == 0))
    def _():
        for h in range(DSA_HEADS):
            g, hh = divmod(h, G)
            for t in range(2):
                bias_ref[t, g, :, hcols(hh)] = _bias_from_buckets(bkt_ref[t], rel_ref, h) * LOG2E
            bias_ref[2, g, :, hcols(hh)] = jnp.full((QB, QB), rel_ref[BUCKET_FAR, h] * LOG2E, F32)

    dq = dq_ref[0] * (DSA_HD ** -0.5 * LOG2E)
    for h in range(DSA_HEADS):
        g, hh = divmod(h, G)
        qt_ref[g, :, hcols(hh)] = dq[:, h * DSA_HD:(h + 1) * DSA_HD].T.astype(BF16)
    iq = iq_ref[0]
    per = LANES // IDX_DIM
    for p in range(IDX_QW // LANES):
        t = iq[:, p * LANES:(p + 1) * LANES].T
        for r in range(per):
            qit_ref[:, hcols(p * per + r)] = t[r * IDX_DIM:(r + 1) * IDX_DIM].astype(BF16)
    smt = sm_ref[0].T
    wrow = jnp.concatenate([smt[SM_IW + h:SM_IW + h + 1, :] for h in range(IDX_HEADS)],
                           axis=1) * (IDX_HEADS ** -0.5 * IDX_DIM ** -0.5)
    assert math.log2(IDX_DIM ** -0.5).is_integer()

    s_rel = lax.broadcasted_iota(I32, (KB, QB), 0)
    q_rel = lax.broadcasted_iota(I32, (KB, QB), 1)
    s_rel2 = lax.broadcasted_iota(I32, (2 * KB, QB), 0)

    nkb = (i * QB + QB + KB - 1) // KB
    npair = (nkb + 1) // 2
    last_blk = vt_ref.shape[2] - 1
    data_off = lambda j: pl.multiple_of(jnp.minimum(j, last_blk) * KB, KB)

    def idx_dot(j):
        return _dot(ik_ref[0, pl.ds(data_off(j), KB), :], qit_ref[...])

    def score_block(j, src_ref, dst_ref):
        dst_ref[...] = idx_dot(j + 1)
        s = jnp.maximum(src_ref[...], 0.0) * wrow
        acc = s[:, hcols(0)]
        for h in range(1, IDX_HEADS):
            acc = acc + s[:, hcols(h)]
        valid = (j * KB + s_rel) <= (i * QB + q_rel)
        keys_ref[pl.ds(pl.multiple_of(j * KB, KB), KB), :] = jnp.where(valid, _to_key(acc), INT_MIN)

    sa_ref[...] = idx_dot(0)

    def scores(t, carry):
        score_block(2 * t, sa_ref, sb_ref)
        score_block(2 * t + 1, sb_ref, sa_ref)
        return carry

    lax.fori_loop(0, npair, scores, 0)

    def fold(fn, op):
        red, comb, init = _FOLD_OPS[op]

        def body(t, acc):
            kt = keys_ref[pl.ds(pl.multiple_of(t * (2 * KB), 2 * KB), 2 * KB), :]
            c = fn(kt, t * (2 * KB) + s_rel2)
            return comb(acc, red(c.reshape(2 * KB // SUBLANES, SUBLANES, QB), axis=0))
        acc = lax.fori_loop(0, npair, body, jnp.full((SUBLANES, QB), init, I32))
        return red(acc, axis=0, keepdims=True)

    n_valid = i * QB + lax.broadcasted_iota(I32, (1, QB), 1) + 1
    tau, pc = _topk_search(fold, ktop, (1, QB), nbits, n_valid)

    m_ref[...] = jnp.full(m_ref.shape, NEG_BIG, F32)
    acc_ref[...] = jnp.zeros(acc_ref.shape, F32)
    sub = KB // QB

    def qk(j, dst_ref):
        for g in range(DSA_KV_HEADS):
            kj = k_ref[0, pl.ds(data_off(j), KB), g * DSA_HD:(g + 1) * DSA_HD]
            dst_ref[g] = _dot(kj, qt_ref[g])

    def attend_block(j, src_ref, dst_ref):
        qk(j + 1, dst_ref)
        sel = _selected(keys_ref[pl.ds(pl.multiple_of(j * KB, KB), KB), :], j * KB + s_rel, tau, pc)
        madd = jnp.where(sel, 0.0, NEG_BIG)
        madd = jnp.concatenate([madd] * G, axis=1)
        alphas, pvs = [], []
        for g in range(DSA_KV_HEADS):
            bias = jnp.concatenate(
                [bias_ref[jnp.clip(i - (j * sub + u), 0, 2), g] for u in range(sub)], axis=0)
            lg = src_ref[g] + bias + madd
            m = m_ref[g]
            mn = jnp.maximum(m, jnp.max(lg, axis=0, keepdims=True))
            alpha = jnp.exp2(m - mn)
            p = jnp.exp2(lg - mn)
            m_ref[g] = mn
            alphas.append(alpha)
            pvs.append(_dot(vt_ref[0, g, jnp.minimum(j, last_blk)], p.astype(BF16)))
        for g in range(DSA_KV_HEADS):
            acc_ref[g] = alphas[g] * acc_ref[g] + pvs[g]

    qk(0, la_ref)

    def attend(t, carry):
        attend_block(2 * t, la_ref, lb_ref)
        attend_block(2 * t + 1, lb_ref, la_ref)
        return carry

    lax.fori_loop(0, npair, attend, 0)
    for h in range(DSA_HEADS):
        g, hh = divmod(h, G)
        out = acc_ref[g, 0:DSA_HD, hcols(hh)] / acc_ref[g, DSA_HD:DSA_HD + 1, hcols(hh)]
        o_ref[0, :, h * DSA_HD:(h + 1) * DSA_HD] = out.T.astype(o_ref.dtype)


def _dsa_prompt(rel_bias, dq, iq, sm, dk, dv, t_real):
    bsz, tp = dq.shape[:2]
    nb = tp // QB
    tk = _round_up(tp, KB)
    nkb = tk // KB
    padk = lambda a: jnp.pad(a.astype(BF16), ((0, 0), (0, tk - tp), (0, 0)))
    ik_bf = padk(sm[:, :, SM_IK:SM_IK + IDX_DIM])
    k_bf = padk(dk)
    vt_bf = padk(dv).reshape(bsz, nkb, KB, DSA_KV_HEADS, DSA_HD).transpose(0, 3, 1, 4, 2)
    vt_bf = jnp.concatenate([vt_bf, jnp.ones(vt_bf.shape[:3] + (SUBLANES, KB), BF16)], axis=3)
    vrows = DSA_HD + SUBLANES
    ktop = min(TOPK_MAX, t_real // 4)
    nbits = int(tk).bit_length()
    gw = DSA_HEADS // DSA_KV_HEADS * QB
    s_rel = np.arange(QB)[:, None]
    q_rel = np.arange(QB)[None, :]
    bkt = np.stack([_rel_bucket_np(q_rel - s_rel), _rel_bucket_np(QB + q_rel - s_rel)])
    return pl.pallas_call(
        functools.partial(_dsa_prompt_kernel, ktop=ktop, nbits=nbits),
        grid=(bsz, nb),
        in_specs=[pl.BlockSpec(memory_space=pltpu.SMEM),
                  pl.BlockSpec((2, QB, QB), lambda b, i: (0, 0, 0)),
                  pl.BlockSpec((1, QB, DSA_QW), lambda b, i: (b, i, 0)),
                  pl.BlockSpec((1, QB, IDX_QW), lambda b, i: (b, i, 0)),
                  pl.BlockSpec((1, QB, LANES), lambda b, i: (b, i, 0)),
                  pl.BlockSpec((1, tk, IDX_DIM), lambda b, i: (b, 0, 0)),
                  pl.BlockSpec((1, tk, DSA_KVW), lambda b, i: (b, 0, 0)),
                  pl.BlockSpec((1, DSA_KV_HEADS, nkb, vrows, KB), lambda b, i: (b, 0, 0, 0, 0))],
        out_specs=pl.BlockSpec((1, QB, DSA_QW), lambda b, i: (b, i, 0)),
        out_shape=jax.ShapeDtypeStruct((bsz, tp, DSA_QW), F32),
        scratch_shapes=[pltpu.VMEM((tk + KB, QB), I32),
                        pltpu.VMEM((3, DSA_KV_HEADS, QB, gw), F32),
                        pltpu.VMEM((DSA_KV_HEADS, DSA_HD, gw), BF16),
                        pltpu.VMEM((IDX_DIM, IDX_HEADS * QB), BF16),
                        pltpu.VMEM((DSA_KV_HEADS, 1, gw), F32),
                        pltpu.VMEM((DSA_KV_HEADS, vrows, gw), F32),
                        pltpu.VMEM((KB, IDX_HEADS * QB), F32),
                        pltpu.VMEM((KB, IDX_HEADS * QB), F32),
                        pltpu.VMEM((DSA_KV_HEADS, KB, gw), F32),
                        pltpu.VMEM((DSA_KV_HEADS, KB, gw), F32)],
        compiler_params=_params(("arbitrary", "arbitrary")),
        name="dsa_prompt",
    )(rel_bias, jnp.asarray(bkt), dq, iq, sm, ik_bf, k_bf, vt_bf)


PP = 8


def _dsa_sample_kernel(pt_ref, rel_ref, bkt_ref, dq_ref, iq_ref, sm_ref, dk_ref, dv_ref, *rest,
                       nq, ns, ktop, nbits, past_len):
    idx_refs = rest[0:PP]
    kp_refs = rest[PP:2 * PP]
    vp_refs = rest[2 * PP:3 * PP]
    o_ref = rest[3 * PP]
    keys_ref, lg_ref, p_ref, acc_ref, linv_ref, qi_ref, qg_ref, wcol_ref = rest[3 * PP + 1:]
    step = pl.program_id(1)
    G = DSA_HEADS // DSA_KV_HEADS
    GR = G * nq
    HR = DSA_HEADS * nq
    SW = PP * PAGE_SIZE
    scale = DSA_HD ** -0.5

    @pl.when(step == 0)
    def _():
        iq = iq_ref[...]
        dq = dq_ref[...]
        sm = sm_ref[...]
        qi_ref[...] = jnp.concatenate(
            [iq[:, h * IDX_DIM:(h + 1) * IDX_DIM] for h in range(IDX_HEADS)], axis=0).astype(BF16)
        qg_ref[...] = jnp.concatenate(
            [dq[:, h * DSA_HD:(h + 1) * DSA_HD] for h in range(DSA_HEADS)], axis=0).astype(BF16)
        wcol = jnp.concatenate(
            [sm[:, SM_IW + h:SM_IW + h + 1] for h in range(IDX_HEADS)], axis=0) * (IDX_HEADS ** -0.5)
        wcol_ref[...] = jnp.broadcast_to(wcol, (HR, LANES))

    def idx_scores(kib, transposed):
        s = _dot(qi_ref[...], kib) if transposed else _dot_nt(qi_ref[...], kib)
        s = jnp.maximum(s * (IDX_DIM ** -0.5), 0.0) * wcol_ref[...]
        return jnp.sum(s.reshape(IDX_HEADS, nq, s.shape[-1]), axis=0)

    def logits(kgs):
        return jnp.concatenate(
            [_dot_nt(qg_ref[g * GR:(g + 1) * GR, :], kgs[g].astype(BF16))
             for g in range(DSA_KV_HEADS)], axis=0) * scale

    page_rows = lambda ref, g: ref[0, pl.ds(g, PAGE_SIZE, stride=DSA_KV_HEADS), :]

    @pl.when(step < ns)
    def _():
        for pi in range(PP):
            lo, hi = pi * PAGE_SIZE, (pi + 1) * PAGE_SIZE
            keys_ref[step, :, lo:hi] = _to_key(idx_scores(idx_refs[pi][0].astype(BF16), True))
            lg_ref[step, :, lo:hi] = logits([page_rows(kp_refs[pi], g) for g in range(DSA_KV_HEADS)])

    @pl.when(step == ns - 1)
    def _():
        zpad = lambda z: jnp.concatenate(
            [z, jnp.zeros((PAGE_SIZE - nq, z.shape[1]), z.dtype)], axis=0)
        sm = sm_ref[...]
        n_idx = lax.broadcasted_iota(I32, (nq, PAGE_SIZE), 1)
        q_idx = lax.broadcasted_iota(I32, (nq, PAGE_SIZE), 0)
        sc_new = idx_scores(zpad(sm[:, SM_IK:SM_IK + IDX_DIM]).astype(BF16), False)
        keys_ref[ns, :, 0:PAGE_SIZE] = jnp.where(n_idx <= q_idx, _to_key(sc_new), INT_MIN)
        dkp = zpad(dk_ref[...])
        lg_ref[ns, :, 0:PAGE_SIZE] = logits([dkp[:, g * DSA_HD:(g + 1) * DSA_HD]
                                             for g in range(DSA_KV_HEADS)])

        lane = lax.broadcasted_iota(I32, (nq, SW), 1)

        def fold(fn, op):
            red, comb, init = _FOLD_OPS[op]
            acc = jnp.full((nq, PAGE_SIZE), init, I32)
            for s in range(ns):
                c = fn(keys_ref[s], s * SW + lane)
                for t in range(PP):
                    acc = comb(acc, c[:, t * PAGE_SIZE:(t + 1) * PAGE_SIZE])
            acc = comb(acc, fn(keys_ref[ns, :, 0:PAGE_SIZE], past_len + n_idx))
            return red(acc, axis=1, keepdims=True)

        n_valid = past_len + 1 + lax.broadcasted_iota(I32, (nq, 1), 0)
        tau, pc = _topk_search(fold, ktop, (nq, 1), nbits, n_valid)

        far = jnp.concatenate([jnp.full((nq, 1), rel_ref[BUCKET_FAR, h], F32)
                               for h in range(DSA_HEADS)], axis=0)
        near = [jnp.concatenate([_bias_from_buckets(bkt_ref[t], rel_ref, h)
                                 for h in range(DSA_HEADS)], axis=0) for t in range(2)]
        tile8 = lambda z: jnp.concatenate([z] * DSA_HEADS, axis=0)

        m = jnp.full((HR, 1), NEG_BIG, F32)
        for s in range(ns + 1):
            if s < ns:
                sel = _selected(keys_ref[s], s * SW + lane, tau, pc)
                bias = far
                lg = lg_ref[s] + tile8(jnp.where(sel, 0.0, NEG_BIG))
                if s == ns - 1:
                    lg = jnp.concatenate(
                        [lg[:, :SW - PAGE_SIZE] + far, lg[:, SW - PAGE_SIZE:] + near[0]], axis=1)
                else:
                    lg = lg + bias
                lg_ref[s] = lg
            else:
                sel = _selected(keys_ref[ns, :, 0:PAGE_SIZE], past_len + n_idx, tau, pc)
                lg = lg_ref[ns, :, 0:PAGE_SIZE] + tile8(jnp.where(sel, 0.0, NEG_BIG)) + near[1]
                lg_ref[ns, :, 0:PAGE_SIZE] = lg
            m = jnp.maximum(m, jnp.max(lg, axis=1, keepdims=True))

        l = jnp.zeros((HR, 1), F32)
        for s in range(ns):
            p = jnp.exp(lg_ref[s] - m)
            l = l + jnp.sum(p, axis=1, keepdims=True)
            p_ref[s] = p.astype(BF16)
        pn = jnp.exp(lg_ref[ns, :, 0:PAGE_SIZE] - m)
        l = l + jnp.sum(pn, axis=1, keepdims=True)
        linv_ref[...] = jnp.broadcast_to(1.0 / l, (HR, LANES))
        vn = zpad(dv_ref[...]).astype(BF16)
        pnb = pn.astype(BF16)
        for g in range(DSA_KV_HEADS):
            acc_ref[g * GR:(g + 1) * GR, :] = _dot(pnb[g * GR:(g + 1) * GR, :],
                                                   vn[:, g * DSA_HD:(g + 1) * DSA_HD])

    @pl.when(step >= ns)
    def _():
        s = step - ns
        for pi in range(PP):
            for g in range(DSA_KV_HEADS):
                acc_ref[g * GR:(g + 1) * GR, :] += _dot(
                    p_ref[s, g * GR:(g + 1) * GR, pi * PAGE_SIZE:(pi + 1) * PAGE_SIZE],
                    page_rows(vp_refs[pi], g).astype(BF16))

    @pl.when(step == 2 * ns - 1)
    def _():
        out = acc_ref[...] * linv_ref[...]
        for h in range(DSA_HEADS):
            o_ref[:, h * DSA_HD:(h + 1) * DSA_HD] = out[h * nq:(h + 1) * nq, :]


def _dsa_sample(page_table, rel_bias, dq, iq, sm, dk, dv, cache_idx, cache_k, cache_v, layer, n_pool):
    db, n_pages = page_table.shape
    nq = dq.shape[0] // db
    assert nq == SUBLANES and n_pages % PP == 0 and nq <= PAGE_SIZE
    ns = n_pages // PP
    past_len = n_pages * PAGE_SIZE
    ktop = min(TOPK_MAX, (past_len + nq) // 4)
    nbits = int(past_len + nq).bit_length()
    base = layer * n_pool
    q_idx = np.arange(nq)[:, None]
    lane = np.arange(PAGE_SIZE)[None, :]
    bkt = np.stack([_rel_bucket_np(PAGE_SIZE + q_idx - lane), _rel_bucket_np(q_idx - lane)])
    hr = DSA_HEADS * nq

    def page_spec(rows, width, phase, pi):
        if phase == 0:
            fn = lambda b, s, pt: (base + pt[b, jnp.minimum(s, ns - 1) * PP + pi], 0, 0)
        else:
            fn = lambda b, s, pt: (base + pt[b, jnp.maximum(s - ns, 0) * PP + pi], 0, 0)
        return pl.BlockSpec((1, rows, width), fn)

    tok = lambda wd: pl.BlockSpec((nq, wd), lambda b, s, pt: (b, 0))
    grid_spec = pltpu.PrefetchScalarGridSpec(
        num_scalar_prefetch=1,
        grid=(db, 2 * ns),
        in_specs=([pl.BlockSpec(memory_space=pltpu.SMEM),
                   pl.BlockSpec((2, nq, PAGE_SIZE), lambda b, s, pt: (0, 0, 0)),
                   tok(DSA_QW), tok(IDX_QW), tok(LANES), tok(DSA_KVW), tok(DSA_KVW)]
                  + [page_spec(IDX_DIM, PAGE_SIZE, 0, pi) for pi in range(PP)]
                  + [page_spec(PAGE_SIZE * DSA_KV_HEADS, DSA_HD, 0, pi) for pi in range(PP)]
                  + [page_spec(PAGE_SIZE * DSA_KV_HEADS, DSA_HD, 1, pi) for pi in range(PP)]),
        out_specs=tok(DSA_QW),
        scratch_shapes=[pltpu.VMEM((ns + 1, nq, PP * PAGE_SIZE), I32),
                        pltpu.VMEM((ns + 1, hr, PP * PAGE_SIZE), F32),
                        pltpu.VMEM((ns, hr, PP * PAGE_SIZE), BF16),
                        pltpu.VMEM((hr, DSA_HD), F32),
                        pltpu.VMEM((hr, LANES), F32),
                        pltpu.VMEM((hr, IDX_DIM), BF16),
                        pltpu.VMEM((hr, DSA_HD), BF16),
                        pltpu.VMEM((hr, LANES), F32)])
    return pl.pallas_call(
        functools.partial(_dsa_sample_kernel, nq=nq, ns=ns, ktop=ktop, nbits=nbits, past_len=past_len),
        grid_spec=grid_spec,
        out_shape=jax.ShapeDtypeStruct((db * nq, DSA_QW), F32),
        compiler_params=_params(("arbitrary", "arbitrary")),
        name="dsa_sample",
    )(page_table, rel_bias, jnp.asarray(bkt), dq, iq, sm, dk, dv,
      *([cache_idx] * PP), *([cache_k] * PP), *([cache_v] * PP))


def _merge_kernel(x_ref, oa_ref, ob_ref, ga_ref, gb_ref, wpa_ref, wpb_ref, wout_ref, gpost_ref, o_ref):
    pa = _dot(oa_ref[...].astype(BF16), wpa_ref[...])
    pb = _dot(ob_ref[...].astype(BF16), wpb_ref[...])
    m = _sigmoid(ga_ref[...]) * pa + _sigmoid(gb_ref[...]) * pb
    mo = _dot(m.astype(BF16), wout_ref[...])
    o_ref[...] = x_ref[...] + _rms(mo, gpost_ref[...])


def _merge(x, oa, ob, ga, gb, wpa, wpb, wout, gpost, tm):
    n = x.shape[0]
    assert n % tm == 0
    row = pl.BlockSpec((tm, D_MODEL), lambda i: (i, 0))
    wsp = pl.BlockSpec((D_MODEL, D_MODEL), lambda i: (0, 0))
    return pl.pallas_call(
        _merge_kernel,
        grid=(n // tm,),
        in_specs=[row, row, row, row, row, wsp, wsp, wsp, pl.BlockSpec((1, D_MODEL), lambda i: (0, 0))],
        out_specs=row,
        out_shape=jax.ShapeDtypeStruct((n, D_MODEL), F32),
        compiler_params=_params(("arbitrary",)),
        name="merge",
    )(x, oa, ob, ga, gb, wpa, wpb, wout, gpost)


FFN_TF = 1024
PREV_ROWS = 16


def _ffn_kernel(x_ref, xprev_ref, gpre_ref, gpost_ref, wg_ref, wu_ref, cw_ref, cb_ref, wd_ref, st_ref,
                o_ref, tail_ref, h2_ref, hp_ref, gext_ref, acc_ref,
                *, shift, nprev, tiles_per_seq, use_state, tail_off):
    i = pl.program_id(0)
    j = pl.program_id(1)
    tm = x_ref.shape[0]

    @pl.when(j == 0)
    def _():
        h2_ref[...] = _rms(x_ref[...], gpre_ref[...]).astype(BF16)
        hp_ref[...] = _rms(xprev_ref[...], gpre_ref[...]).astype(BF16)
        acc_ref[...] = jnp.zeros_like(acc_ref)

    h2 = h2_ref[...]
    g = _dot(h2, wg_ref[...])
    u = _dot(h2, wu_ref[...])
    if use_state:
        gprev = st_ref[...]
    else:
        gprev = _dot(hp_ref[...], wg_ref[...])
        gprev = jnp.where(i % tiles_per_seq == 0, 0.0, gprev)
    gext_ref[0:nprev, :] = gprev
    gext_ref[nprev:nprev + tm, :] = g
    cw = cw_ref[...]
    c = (cb_ref[...] + gext_ref[nprev - 2 * shift:nprev - 2 * shift + tm, :] * cw[0:1, :]
         + gext_ref[nprev - shift:nprev - shift + tm, :] * cw[1:2, :] + g * cw[2:3, :])
    gelu = 0.5 * c * (1.0 + jnp.tanh(math.sqrt(2.0 / math.pi) * (c + 0.044715 * (c * c * c))))
    acc_ref[...] += _dot((gelu * u).astype(BF16), wd_ref[...])
    tail_ref[...] = g[tail_off:tail_off + tail_ref.shape[0], :]

    @pl.when(j == pl.num_programs(1) - 1)
    def _():
        o_ref[...] = x_ref[...] + _rms(acc_ref[...], gpost_ref[...])


def _ffn(x, gpre, gpost, w_in_bf, conv_w, conv_b, w_down_bf, state, *, tm, shift, tiles_per_seq,
         use_state, tail_off, tail_rows):
    n = x.shape[0]
    assert n % tm == 0 and tm % PREV_ROWS == 0 and D_FF % FFN_TF == 0
    nprev = state.shape[0] if use_state else PREV_ROWS
    assert nprev >= 2 * shift and (not use_state or n == tm)
    nj = D_FF // FFN_TF
    pr = tm // PREV_ROWS
    return pl.pallas_call(
        functools.partial(_ffn_kernel, shift=shift, nprev=nprev, tiles_per_seq=tiles_per_seq,
                          use_state=use_state, tail_off=tail_off),
        grid=(n // tm, nj),
        in_specs=[pl.BlockSpec((tm, D_MODEL), lambda i, j: (i, 0)),
                  pl.BlockSpec((PREV_ROWS, D_MODEL), lambda i, j: (jnp.maximum(i * pr - 1, 0), 0)),
                  pl.BlockSpec((1, D_MODEL), lambda i, j: (0, 0)),
                  pl.BlockSpec((1, D_MODEL), lambda i, j: (0, 0)),
                  pl.BlockSpec((D_MODEL, FFN_TF), lambda i, j: (0, j)),
                  pl.BlockSpec((D_MODEL, FFN_TF), lambda i, j: (0, j + nj)),
                  pl.BlockSpec((CONV_W, FFN_TF), lambda i, j: (0, j)),
                  pl.BlockSpec((1, FFN_TF), lambda i, j: (0, j)),
                  pl.BlockSpec((FFN_TF, D_MODEL), lambda i, j: (j, 0)),
                  pl.BlockSpec((state.shape[0], FFN_TF), lambda i, j: (0, j))],
        out_specs=[pl.BlockSpec((tm, D_MODEL), lambda i, j: (i, 0)),
                   pl.BlockSpec((tail_rows, FFN_TF), lambda i, j: (i, j))],
        out_shape=[jax.ShapeDtypeStruct((n, D_MODEL), F32),
                   jax.ShapeDtypeStruct((n // tm * tail_rows, D_FF), F32)],
        scratch_shapes=[pltpu.VMEM((tm, D_MODEL), BF16),
                        pltpu.VMEM((PREV_ROWS, D_MODEL), BF16),
                        pltpu.VMEM((nprev + tm, FFN_TF), F32),
                        pltpu.VMEM((tm, D_MODEL), F32)],
        compiler_params=_params(("arbitrary", "arbitrary")),
        name="ffn",
    )(x, x, gpre, gpost, w_in_bf, w_in_bf, conv_w, conv_b, w_down_bf, state)


def _round_up(a, m):
    return -(-a // m) * m


def kernel(x_prompt, x_sample, cache_k, cache_v, cache_idx_k, state_gla, state_conv, page_table, meta_tokens, rel_bias, norm_mix_pre, norm_mix_post, norm_ffn_pre, norm_ffn_post, w_in, w_a2, b_a, gla_norm, w_pa, w_pb, w_out, w_ffn_in, conv_w, conv_b, w_ffn_down):
    bsz, seq = x_prompt.shape[:2]
    db, nq = x_sample.shape[:2]
    depth = w_in.shape[0]
    n_pool = cache_k.shape[1]
    t_real = seq + N_META
    tp = _round_up(t_real, QB)
    nb = tp // QB
    row2 = lambda a: a.reshape(1, -1)
    seg_names = [n for n, _ in SEGS] + ["bcum"]

    w_in_r = [_relayout_w_in(w_in[l]) for l in range(depth)]
    w_pa_b, w_pb_b, w_out_b = w_pa.astype(BF16), w_pb.astype(BF16), w_out.astype(BF16)
    w_ffn_in_b, w_ffn_down_b = w_ffn_in.astype(BF16), w_ffn_down.astype(BF16)

    ck = cache_k.reshape(depth * n_pool, PAGE_SIZE * DSA_KV_HEADS, DSA_HD)
    cv = cache_v.reshape(depth * n_pool, PAGE_SIZE * DSA_KV_HEADS, DSA_HD)
    ci = cache_idx_k.transpose(0, 1, 3, 2).reshape(depth * n_pool, IDX_DIM, PAGE_SIZE)

    xp = jnp.concatenate([jnp.broadcast_to(meta_tokens[None].astype(x_prompt.dtype), (bsz, N_META, D_MODEL)),
                          x_prompt, jnp.zeros((bsz, tp - t_real, D_MODEL), x_prompt.dtype)], axis=1)
    xp = xp.reshape(bsz * tp, D_MODEL)
    ffn_tm = tp // 4 if (tp // 4) % PREV_ROWS == 0 else tp
    tiles_per_seq = tp // ffn_tm
    tail_pos = (t_real - (CONV_W - 1)) % ffn_tm
    tail_off = tail_pos // SUBLANES * SUBLANES
    assert tail_pos - tail_off + (CONV_W - 1) <= SUBLANES
    pk, pv, pik, pgla, pconv = [], [], [], [], []
    zero_state = jnp.zeros((bsz, GLA_HEADS, GLA_DK, GLA_DV), F32)
    zero_conv = jnp.zeros((PREV_ROWS, D_FF), F32)
    for l in range(depth):
        z = dict(zip(seg_names, _proj(xp, row2(norm_mix_pre[l]), w_in_r[l], w_a2[l], row2(b_a[l]),
                                      2 * QB, QB, tp, t_real)))
        r3 = lambda a: a.reshape(bsz, tp, a.shape[-1])
        o_a, s_fin = _gla(r3(z["gq"]), r3(z["gk"]), r3(z["gv"]), r3(z["gr"]), r3(z["bcum"]),
                          row2(gla_norm[l]), zero_state, 0, t_real)
        dk3, dv3, sm3 = r3(z["dk"]), r3(z["dv"]), r3(z["sm"])
        o_b = _dsa_prompt(rel_bias, r3(z["dq"]), r3(z["iq"]), sm3, dk3, dv3, t_real)
        xm = _merge(xp, o_a.reshape(bsz * tp, GLA_VW), o_b.reshape(bsz * tp, DSA_QW), z["ga"], z["gb"],
                    w_pa_b[l], w_pb_b[l], w_out_b[l], row2(norm_mix_post[l]), 512 if (bsz * tp) % 512 == 0 else QB)
        xp, tail = _ffn(xm, row2(norm_ffn_pre[l]), row2(norm_ffn_post[l]), w_ffn_in_b[l], conv_w[l],
                        row2(conv_b[l]), w_ffn_down_b[l], zero_conv, tm=ffn_tm, shift=1,
                        tiles_per_seq=tiles_per_seq, use_state=False, tail_off=tail_off, tail_rows=SUBLANES)
        pk.append(dk3[:, :t_real].reshape(bsz, t_real, DSA_KV_HEADS, DSA_HD))
        pv.append(dv3[:, :t_real].reshape(bsz, t_real, DSA_KV_HEADS, DSA_HD))
        pik.append(sm3[:, :t_real, SM_IK:SM_IK + IDX_DIM])
        pgla.append(s_fin)
        tail = tail.reshape(bsz, tiles_per_seq, SUBLANES, D_FF)[:, (t_real - 1) // ffn_tm]
        pconv.append(tail[:, tail_pos - tail_off:tail_pos - tail_off + CONV_W - 1])
    y_prompt = xp.reshape(bsz, tp, D_MODEL)[:, N_META:t_real]

    ns_rows = db * nq
    xs = x_sample.reshape(ns_rows, D_MODEL)
    sk, sv, sik, sgla, sconv = [], [], [], [], []
    state_all = state_gla.reshape(depth * db, GLA_HEADS, GLA_DK, GLA_DV)
    for l in range(depth):
        z = dict(zip(seg_names, _proj(xs, row2(norm_mix_pre[l]), w_in_r[l], w_a2[l], row2(b_a[l]),
                                      ns_rows, nq, nq, nq)))
        r3 = lambda a: a.reshape(db, nq, a.shape[-1])
        o_a, s_fin = _gla(r3(z["gq"]), r3(z["gk"]), r3(z["gv"]), r3(z["gr"]), r3(z["bcum"]),
                          row2(gla_norm[l]), state_all, l * db, nq)
        o_b = _dsa_sample(page_table, rel_bias, z["dq"], z["iq"], z["sm"], z["dk"], z["dv"],
                          ci, ck, cv, l, n_pool)
        xm = _merge(xs, o_a.reshape(ns_rows, GLA_VW), o_b, z["ga"], z["gb"],
                    w_pa_b[l], w_pb_b[l], w_out_b[l], row2(norm_mix_post[l]), ns_rows)
        xm_t = xm.reshape(db, nq, D_MODEL).transpose(1, 0, 2).reshape(ns_rows, D_MODEL)
        st = state_conv[l].transpose(1, 0, 2).reshape((CONV_W - 1) * db, D_FF)
        xo_t, tail = _ffn(xm_t, row2(norm_ffn_pre[l]), row2(norm_ffn_post[l]), w_ffn_in_b[l], conv_w[l],
                          row2(conv_b[l]), w_ffn_down_b[l], st, tm=ns_rows, shift=db, tiles_per_seq=1,
                          use_state=True, tail_off=(nq - (CONV_W - 1)) * db, tail_rows=(CONV_W - 1) * db)
        xs = xo_t.reshape(nq, db, D_MODEL).transpose(1, 0, 2).reshape(ns_rows, D_MODEL)
        sk.append(z["dk"].reshape(db, nq, DSA_KV_HEADS, DSA_HD))
        sv.append(z["dv"].reshape(db, nq, DSA_KV_HEADS, DSA_HD))
        sik.append(z["sm"][:, SM_IK:SM_IK + IDX_DIM].reshape(db, nq, IDX_DIM))
        sgla.append(s_fin)
        sconv.append(tail.reshape(CONV_W - 1, db, D_FF).transpose(1, 0, 2))
    y_sample = xs.reshape(db, nq, D_MODEL)

    return (y_prompt, y_sample,
            jnp.stack(pk), jnp.stack(pv), jnp.stack(pik), jnp.stack(pgla), jnp.stack(pconv),
            jnp.stack(sk), jnp.stack(sv), jnp.stack(sik), jnp.stack(sgla), jnp.stack(sconv))
```

```python
import functools
import math

import jax
import jax.numpy as jnp
import numpy as np
from jax import lax
from jax.experimental import pallas as pl
from jax.experimental.pallas import tpu as pltpu

F32 = jnp.float32
BF16 = jnp.bfloat16
I32 = jnp.int32

D_MODEL = 1024
N_META = 16
GLA_HEADS = 4
GLA_DK = 128
GLA_DV = 256
GLA_RANK = 16
GLA_TAU = 16.0
DSA_HEADS = 8
DSA_KV_HEADS = 2
DSA_HD = 128
IDX_HEADS = 8
IDX_DIM = 64
TOPK_MAX = 256
REL_BUCKETS = 32
REL_MAX_DIST = 128
D_FF = 4096
CONV_W = 3
EPS = 1e-6
PAGE_SIZE = 128

GLA_KW = GLA_HEADS * GLA_DK
GLA_VW = GLA_HEADS * GLA_DV
DSA_QW = DSA_HEADS * DSA_HD
DSA_KVW = DSA_KV_HEADS * DSA_HD
IDX_QW = IDX_HEADS * IDX_DIM

LANES = 128
SUBLANES = 8
CHUNK = 128
KB = 256
QB = 128
COUNT_BLOCKS = 2
KEYS_PAD_BLOCKS = 3
LOG2E = math.log2(math.e)
VMEM_LIMIT = 56 * 1024 * 1024

INT_MIN = -(2 ** 31)
NEG_BIG = -1e30

SEGS = (("gq", GLA_KW), ("gk", GLA_KW), ("gv", GLA_VW), ("gr", GLA_VW), ("dq", DSA_QW),
        ("dk", DSA_KVW), ("dv", DSA_KVW), ("iq", IDX_QW), ("ga", D_MODEL), ("gb", D_MODEL),
        ("sm", LANES))
SM_IK = 0
SM_GA1 = IDX_DIM
SM_IW = IDX_DIM + GLA_RANK
SEG_OFFS = tuple(int(v) for v in np.cumsum([0] + [w for _, w in SEGS]))
PW = SEG_OFFS[-1]


def _relayout_w_in(w):
    o = np.cumsum([0, GLA_KW, GLA_KW, GLA_VW, GLA_VW, GLA_RANK, DSA_QW, DSA_KVW, DSA_KVW,
                   IDX_QW, IDX_HEADS, IDX_DIM, D_MODEL, D_MODEL]).tolist()
    c = lambda i: w[:, o[i]:o[i + 1]]
    pad = jnp.zeros((w.shape[0], LANES - IDX_DIM - GLA_RANK - IDX_HEADS), w.dtype)
    cols = [c(0), c(1), c(2), c(3), c(5), c(6), c(7), c(8), c(11), c(12), c(10), c(4), c(9), pad]
    return jnp.concatenate(cols, axis=1).astype(BF16)


def _rms(x, g):
    return x * lax.rsqrt(jnp.mean(x * x, axis=-1, keepdims=True) + EPS) * g


def _sigmoid(x):
    return 1.0 / (1.0 + jnp.exp(-x))


def _dot(a, b):
    return jnp.dot(a, b, preferred_element_type=F32)


def _dot_nt(a, b):
    return lax.dot_general(a, b, (((1,), (1,)), ((), ())), preferred_element_type=F32)


def _params(sem):
    return pltpu.CompilerParams(dimension_semantics=sem, vmem_limit_bytes=VMEM_LIMIT)


def _split3(z):
    hi = z.astype(BF16)
    r = z - hi.astype(F32)
    mid = r.astype(BF16)
    return hi, mid, (r - mid.astype(F32)).astype(BF16)


def _proj_kernel(x_ref, g_ref, w_ref, wa2_ref, ba_ref, *out_refs, chunk, seq_rows, t_valid):
    tm = x_ref.shape[0]
    hb = _rms(x_ref[...], g_ref[...]).astype(BF16)
    segment = lambda s: _dot(hb, w_ref[:, SEG_OFFS[s]:SEG_OFFS[s + 1]])
    assert SEGS[-1][0] == "sm"
    seg = segment(len(SEGS) - 1)
    out_refs[len(SEGS) - 1][...] = seg
    ga1 = seg[:, SM_GA1:SM_GA1 + GLA_RANK]
    x = jnp.dot(ga1, wa2_ref[...], precision=lax.Precision.HIGHEST,
                preferred_element_type=F32) + ba_ref[...]
    la = (jnp.minimum(x, 0.0) - jnp.log1p(jnp.exp(-jnp.abs(x)))) * (1.0 / GLA_TAU)
    row = pl.program_id(0) * tm + lax.broadcasted_iota(I32, (tm, 1), 0)
    la = jnp.where(row % seq_rows < t_valid, la, 0.0)
    ti = lax.broadcasted_iota(I32, (tm, tm), 0)
    si = lax.broadcasted_iota(I32, (tm, tm), 1)
    tri = jnp.where((ti // chunk == si // chunk) & (ti >= si), 1.0, 0.0).astype(BF16)
    out_refs[-1][...] = sum(_dot(tri, piece) for piece in _split3(la))
    for s in range(len(SEGS) - 1):
        out_refs[s][...] = segment(s)


def _proj(x, g, w, wa2, ba, tm, chunk, seq_rows, t_valid):
    n = x.shape[0]
    assert n % tm == 0 and tm % chunk == 0 and seq_rows % chunk == 0
    widths = [wd for _, wd in SEGS] + [GLA_KW]
    full = lambda a: pl.BlockSpec(a.shape, lambda i: (0, 0))
    return pl.pallas_call(
        functools.partial(_proj_kernel, chunk=chunk, seq_rows=seq_rows, t_valid=t_valid),
        grid=(n // tm,),
        in_specs=[pl.BlockSpec((tm, D_MODEL), lambda i: (i, 0)), full(g), full(w), full(wa2), full(ba)],
        out_specs=[pl.BlockSpec((tm, wd), lambda i: (i, 0)) for wd in widths],
        out_shape=[jax.ShapeDtypeStruct((n, wd), F32) for wd in widths],
        compiler_params=_params(("arbitrary",)),
        name="proj",
    )(x, g, w, wa2, ba)


def _gla_kernel(q_ref, k_ref, v_ref, gr_ref, bc_ref, gain_ref, s0_ref,
                o_ref, sfin_ref, s_ref, *, t_valid, rows):
    C = CHUNK
    c = pl.program_id(2)

    @pl.when(c == 0)
    def _():
        s_ref[...] = s0_ref[0, 0]

    def padrows(z):
        if rows == C:
            return z
        return jnp.concatenate([z, jnp.zeros((C - rows, z.shape[1]), z.dtype)], axis=0)

    rowi = lax.broadcasted_iota(I32, (C, 1), 0)
    live = c * C + rowi < t_valid
    q = padrows(q_ref[0]) * (GLA_DK ** -0.5)
    k = jnp.where(live, padrows(k_ref[0]), 0.0)
    v = padrows(v_ref[0])
    ti = lax.broadcasted_iota(I32, (C, C), 0)
    si = lax.broadcasted_iota(I32, (C, C), 1)
    b = bc_ref[0]
    if rows < C:
        b = jnp.concatenate([b, jnp.broadcast_to(b[rows - 1:rows, :], (C - rows, GLA_DK))], axis=0)

    att = None
    m = C // 2
    while m >= SUBLANES:
        pieces = []
        for p in range(C // (2 * m)):
            r = p * 2 * m + m - 1
            pieces.append(jnp.broadcast_to(b[r:r + 1, :], (2 * m, GLA_DK)))
        bound = pieces[0] if len(pieces) == 1 else jnp.concatenate(pieces, axis=0)
        upper = ((rowi // m) % 2) == 1
        e = jnp.exp(-jnp.abs(b - bound))
        qm = jnp.where(upper, q * e, 0.0)
        km = jnp.where(upper, 0.0, k * e)
        a = _dot_nt(qm.astype(BF16), km.astype(BF16))
        if 2 * m < C:
            a = jnp.where((ti // (2 * m)) == (si // (2 * m)), a, 0.0)
        att = a if att is None else att + a
        m //= 2

    tm8 = rowi % SUBLANES
    diag = ti - si
    for delta in range(SUBLANES):
        kd = k if delta == 0 else pltpu.roll(k, delta, 0)
        bd = b if delta == 0 else pltpu.roll(b, delta, 0)
        w = q * kd * jnp.exp(jnp.minimum(b - bd, 0.0))
        a = jnp.sum(w, axis=1, keepdims=True)
        att = jnp.where((diag == delta) & (tm8 >= delta), a, att)

    s = s_ref[...]
    vb = v.astype(BF16)
    inter = _dot((q * jnp.exp(b)).astype(BF16), s.astype(BF16))
    o = inter + _dot(att.astype(BF16), vb)

    b_last_row = b[C - 1:C, :]
    b_last_col = b.T[:, C - 1:C]
    kdec = k * jnp.exp(b_last_row - b)
    s_new = s * jnp.exp(b_last_col) + _dot(kdec.T.astype(BF16), vb)
    s_ref[...] = s_new

    on = o * lax.rsqrt(jnp.mean(o * o, axis=-1, keepdims=True) + EPS) * gain_ref[...]
    gr = padrows(gr_ref[0])
    res = on * (gr * _sigmoid(gr))
    o_ref[0] = res[:rows].astype(o_ref.dtype)

    @pl.when(c == pl.num_programs(2) - 1)
    def _():
        sfin_ref[0, 0] = s_new


def _gla(gq, gk, gv, gr, bcum, gain, s0, s0_first, t_valid):
    bsz, tp = gq.shape[:2]
    rows = min(tp, CHUNK)
    assert tp % rows == 0
    nc = tp // rows
    tok = lambda wd: pl.BlockSpec((1, rows, wd), lambda b, h, c: (b, c, h))
    return pl.pallas_call(
        functools.partial(_gla_kernel, t_valid=t_valid, rows=rows),
        grid=(bsz, GLA_HEADS, nc),
        in_specs=[tok(GLA_DK), tok(GLA_DK), tok(GLA_DV), tok(GLA_DV), tok(GLA_DK),
                  pl.BlockSpec((1, GLA_DV), lambda b, h, c: (0, 0)),
                  pl.BlockSpec((1, 1, GLA_DK, GLA_DV), lambda b, h, c: (s0_first + b, h, 0, 0))],
        out_specs=[pl.BlockSpec((1, rows, GLA_DV), lambda b, h, c: (b, c, h)),
                   pl.BlockSpec((1, 1, GLA_DK, GLA_DV), lambda b, h, c: (b, h, 0, 0))],
        out_shape=[jax.ShapeDtypeStruct((bsz, tp, GLA_VW), F32),
                   jax.ShapeDtypeStruct((bsz, GLA_HEADS, GLA_DK, GLA_DV), F32)],
        scratch_shapes=[pltpu.VMEM((GLA_DK, GLA_DV), F32)],
        compiler_params=_params(("arbitrary", "arbitrary", "arbitrary")),
        name="gla",
    )(gq, gk, gv, gr, bcum, gain, s0)


def _rel_bucket_np(dist):
    max_exact = REL_BUCKETS // 2
    d = np.maximum(dist, 0)
    large = max_exact + (np.log(np.maximum(d, 1).astype(np.float64) / max_exact)
                         / math.log(REL_MAX_DIST / max_exact) * (REL_BUCKETS - max_exact)).astype(np.int64)
    large = np.minimum(large, REL_BUCKETS - 1)
    return np.where(d < max_exact, d, large).astype(np.int32)


BUCKET_FAR = int(_rel_bucket_np(np.array([REL_MAX_DIST]))[0])
assert (_rel_bucket_np(np.arange(REL_MAX_DIST, 4 * 8192)) == BUCKET_FAR).all()
assert REL_MAX_DIST <= PAGE_SIZE and REL_MAX_DIST <= KB


def _to_key(x):
    bits = lax.bitcast_convert_type(x + 0.0, I32)
    return jnp.where(bits < 0, bits ^ 0x7FFFFFFF, bits)


def _bias_from_buckets(bkt, rel_ref, h):
    acc = jnp.zeros(bkt.shape, F32)
    for bb in range(REL_BUCKETS):
        acc = jnp.where(bkt == bb, rel_ref[bb, h], acc)
    return acc


_FOLD_OPS = {"sum": (jnp.sum, jnp.add, 0)}
BITS_PER_TRIP = 4


def _topk_search(fold, ktop, shape, nbits, n_valid):
    count = lambda pred: fold(lambda kt, pos: jnp.where(pred(kt, pos), 1, 0).astype(I32), "sum")

    def pending(fin):
        return jnp.max(jnp.where(fin > 0, 0.0, 1.0)) > 0.0

    cnt0 = count(lambda kt, pos: kt >= 0)
    tau = jnp.where(cnt0 >= ktop, jnp.zeros(shape, I32), jnp.full(shape, INT_MIN, I32))
    cge = jnp.where(cnt0 >= ktop, cnt0, n_valid + jnp.zeros(shape, I32))

    def one_bit(it, st):
        tau, cge = st
        cand = tau | lax.shift_left(jnp.int32(1), 30 - it)
        cnt = count(lambda kt, pos: kt >= cand)
        up = cnt >= ktop
        return jnp.where(up, cand, tau), jnp.where(up, cnt, cge)

    lead = 31 % BITS_PER_TRIP

    def bits(trip, st):
        for u in range(BITS_PER_TRIP):
            st = one_bit(lead + trip * BITS_PER_TRIP + u, st)
        return st

    st = (tau, cge)
    for it in range(lead):
        st = one_bit(it, st)
    tau, cge = lax.fori_loop(0, 31 // BITS_PER_TRIP, bits, st)
    few = n_valid <= ktop
    tau = jnp.where(few, INT_MIN, tau)
    fin = jnp.where(few | (cge == ktop), 1, 0)

    def tie_search():
        need = ktop - count(lambda kt, pos: kt > tau)

        def pos_body(it, pc):
            cand = pc | lax.shift_left(jnp.int32(1), nbits - 1 - it)
            cnt = count(lambda kt, pos: (kt == tau) & (pos < cand))
            return jnp.where(cnt <= need, cand, pc)

        return lax.fori_loop(0, nbits, pos_body, jnp.zeros(shape, I32))

    pc = lax.cond(pending(fin), tie_search, lambda: jnp.full(shape, (1 << nbits) - 1, I32))
    return tau, pc


def _selected(kt, pos, tau, pc):
    return (kt > tau) | ((kt == tau) & (pos < pc) & (tau > INT_MIN))


def _dsa_prompt_kernel(rel_ref, bkt_ref, dq_ref, iq_ref, sm_ref, ik_ref, k_ref, vt_ref, o_ref,
                       keys_ref, bias_ref, qt_ref, qit_ref, m_ref, acc_ref,
                       sa_ref, sb_ref, la_ref, lb_ref, *, ktop, nbits):
    bi = pl.program_id(0)
    i = pl.program_id(1)
    G = DSA_HEADS // DSA_KV_HEADS
    hcols = lambda hh: slice(hh * QB, (hh + 1) * QB)

    @pl.when((bi == 0) & (i == 0))
    def _():
        for h in range(DSA_HEADS):
            g, hh = divmod(h, G)
            for t in range(2):
                bias_ref[t, g, :, hcols(hh)] = _bias_from_buckets(bkt_ref[t], rel_ref, h) * LOG2E
            bias_ref[2, g, :, hcols(hh)] = jnp.full((QB, QB), rel_ref[BUCKET_FAR, h] * LOG2E, F32)

    dq = dq_ref[0] * (DSA_HD ** -0.5 * LOG2E)
    for h in range(DSA_HEADS):
        g, hh = divmod(h, G)
        qt_ref[g, :, hcols(hh)] = dq[:, h * DSA_HD:(h + 1) * DSA_HD].T.astype(BF16)
    iq = iq_ref[0]
    per = LANES // IDX_DIM
    for p in range(IDX_QW // LANES):
        t = iq[:, p * LANES:(p + 1) * LANES].T
        for r in range(per):
            qit_ref[:, hcols(p * per + r)] = t[r * IDX_DIM:(r + 1) * IDX_DIM].astype(BF16)
    smt = sm_ref[0].T
    wrow = jnp.concatenate([smt[SM_IW + h:SM_IW + h + 1, :] for h in range(IDX_HEADS)],
                           axis=1) * (IDX_HEADS ** -0.5 * IDX_DIM ** -0.5)
    assert math.log2(IDX_DIM ** -0.5).is_integer()

    s_rel = lax.broadcasted_iota(I32, (KB, QB), 0)
    q_rel = lax.broadcasted_iota(I32, (KB, QB), 1)
    s_rel2 = lax.broadcasted_iota(I32, (2 * KB, QB), 0)

    nkb = (i * QB + QB + KB - 1) // KB
    npair = (nkb + 1) // 2
    last_blk = vt_ref.shape[2] - 1
    data_off = lambda j: pl.multiple_of(jnp.minimum(j, last_blk) * KB, KB)

    def idx_dot(j):
        return _dot(ik_ref[0, pl.ds(data_off(j), KB), :], qit_ref[...])

    def score_block(j, src_ref, dst_ref):
        dst_ref[...] = idx_dot(j + 1)
        s = jnp.maximum(src_ref[...], 0.0) * wrow
        acc = s[:, hcols(0)]
        for h in range(1, IDX_HEADS):
            acc = acc + s[:, hcols(h)]
        valid = (j * KB + s_rel) <= (i * QB + q_rel)
        keys_ref[pl.ds(pl.multiple_of(j * KB, KB), KB), :] = jnp.where(valid, _to_key(acc), INT_MIN)

    sa_ref[...] = idx_dot(0)

    def scores(t, carry):
        score_block(2 * t, sa_ref, sb_ref)
        score_block(2 * t + 1, sb_ref, sa_ref)
        return carry

    lax.fori_loop(0, npair, scores, 0)

    span = 2 * KB
    keys_ref[pl.ds(pl.multiple_of(npair * span, span), span), :] = jnp.full((span, QB), INT_MIN, I32)
    per_trip = COUNT_BLOCKS // 2
    ntrip = (npair + per_trip - 1) // per_trip

    def fold(fn, op):
        red, comb, init = _FOLD_OPS[op]

        def body(t, acc):
            for u in range(per_trip):
                off = (t * per_trip + u) * span
                kt = keys_ref[pl.ds(pl.multiple_of(off, span), span), :]
                c = fn(kt, off + s_rel2)
                acc = comb(acc, red(c.reshape(span // SUBLANES, SUBLANES, QB), axis=0))
            return acc
        acc = lax.fori_loop(0, ntrip, body, jnp.full((SUBLANES, QB), init, I32))
        return red(acc, axis=0, keepdims=True)

    n_valid = i * QB + lax.broadcasted_iota(I32, (1, QB), 1) + 1
    tau, pc = _topk_search(fold, ktop, (1, QB), nbits, n_valid)

    m_ref[...] = jnp.full(m_ref.shape, NEG_BIG, F32)
    acc_ref[...] = jnp.zeros(acc_ref.shape, F32)
    sub = KB // QB

    def qk(j, dst_ref):
        for g in range(DSA_KV_HEADS):
            kj = k_ref[0, pl.ds(data_off(j), KB), g * DSA_HD:(g + 1) * DSA_HD]
            dst_ref[g] = _dot(kj, qt_ref[g])

    def attend_block(j, src_ref, dst_ref):
        qk(j + 1, dst_ref)
        sel = _selected(keys_ref[pl.ds(pl.multiple_of(j * KB, KB), KB), :], j * KB + s_rel, tau, pc)
        madd = jnp.where(sel, 0.0, NEG_BIG)
        madd = jnp.concatenate([madd] * G, axis=1)
        alphas, pvs = [], []
        for g in range(DSA_KV_HEADS):
            bias = jnp.concatenate(
                [bias_ref[jnp.clip(i - (j * sub + u), 0, 2), g] for u in range(sub)], axis=0)
            lg = src_ref[g] + bias + madd
            m = m_ref[g]
            mn = jnp.maximum(m, jnp.max(lg, axis=0, keepdims=True))
            alpha = jnp.exp2(m - mn)
            p = jnp.exp2(lg - mn)
            m_ref[g] = mn
            alphas.append(alpha)
            pvs.append(_dot(vt_ref[0, g, jnp.minimum(j, last_blk)], p.astype(BF16)))
        for g in range(DSA_KV_HEADS):
            acc_ref[g] = alphas[g] * acc_ref[g] + pvs[g]

    qk(0, la_ref)

    def attend(t, carry):
        attend_block(2 * t, la_ref, lb_ref)
        attend_block(2 * t + 1, lb_ref, la_ref)
        return carry

    lax.fori_loop(0, npair, attend, 0)
    for h in range(DSA_HEADS):
        g, hh = divmod(h, G)
        out = acc_ref[g, 0:DSA_HD, hcols(hh)] / acc_ref[g, DSA_HD:DSA_HD + 1, hcols(hh)]
        o_ref[0, :, h * DSA_HD:(h + 1) * DSA_HD] = out.T.astype(o_ref.dtype)


def _dsa_prompt(rel_bias, dq, iq, sm, dk, dv, t_real):
    bsz, tp = dq.shape[:2]
    nb = tp // QB
    tk = _round_up(tp, KB)
    nkb = tk // KB
    padk = lambda a: jnp.pad(a.astype(BF16), ((0, 0), (0, tk - tp), (0, 0)))
    ik_bf = padk(sm[:, :, SM_IK:SM_IK + IDX_DIM])
    k_bf = padk(dk)
    vt_bf = padk(dv).reshape(bsz, nkb, KB, DSA_KV_HEADS, DSA_HD).transpose(0, 3, 1, 4, 2)
    vt_bf = jnp.concatenate([vt_bf, jnp.ones(vt_bf.shape[:3] + (SUBLANES, KB), BF16)], axis=3)
    vrows = DSA_HD + SUBLANES
    ktop = min(TOPK_MAX, t_real // 4)
    nbits = int(tk).bit_length()
    gw = DSA_HEADS // DSA_KV_HEADS * QB
    s_rel = np.arange(QB)[:, None]
    q_rel = np.arange(QB)[None, :]
    bkt = np.stack([_rel_bucket_np(q_rel - s_rel), _rel_bucket_np(QB + q_rel - s_rel)])
    return pl.pallas_call(
        functools.partial(_dsa_prompt_kernel, ktop=ktop, nbits=nbits),
        grid=(bsz, nb),
        in_specs=[pl.BlockSpec(memory_space=pltpu.SMEM),
                  pl.BlockSpec((2, QB, QB), lambda b, i: (0, 0, 0)),
                  pl.BlockSpec((1, QB, DSA_QW), lambda b, i: (b, i, 0)),
                  pl.BlockSpec((1, QB, IDX_QW), lambda b, i: (b, i, 0)),
                  pl.BlockSpec((1, QB, LANES), lambda b, i: (b, i, 0)),
                  pl.BlockSpec((1, tk, IDX_DIM), lambda b, i: (b, 0, 0)),
                  pl.BlockSpec((1, tk, DSA_KVW), lambda b, i: (b, 0, 0)),
                  pl.BlockSpec((1, DSA_KV_HEADS, nkb, vrows, KB), lambda b, i: (b, 0, 0, 0, 0))],
        out_specs=pl.BlockSpec((1, QB, DSA_QW), lambda b, i: (b, i, 0)),
        out_shape=jax.ShapeDtypeStruct((bsz, tp, DSA_QW), F32),
        scratch_shapes=[pltpu.VMEM((tk + KEYS_PAD_BLOCKS * KB, QB), I32),
                        pltpu.VMEM((3, DSA_KV_HEADS, QB, gw), F32),
                        pltpu.VMEM((DSA_KV_HEADS, DSA_HD, gw), BF16),
                        pltpu.VMEM((IDX_DIM, IDX_HEADS * QB), BF16),
                        pltpu.VMEM((DSA_KV_HEADS, 1, gw), F32),
                        pltpu.VMEM((DSA_KV_HEADS, vrows, gw), F32),
                        pltpu.VMEM((KB, IDX_HEADS * QB), F32),
                        pltpu.VMEM((KB, IDX_HEADS * QB), F32),
                        pltpu.VMEM((DSA_KV_HEADS, KB, gw), F32),
                        pltpu.VMEM((DSA_KV_HEADS, KB, gw), F32)],
        compiler_params=_params(("arbitrary", "arbitrary")),
        name="dsa_prompt",
    )(rel_bias, jnp.asarray(bkt), dq, iq, sm, ik_bf, k_bf, vt_bf)


PP = 16


def _dsa_sample_kernel(pt_ref, rel_ref, bkt_ref, dq_ref, iq_ref, sm_ref, dk_ref, dv_ref, *rest,
                       nq, ns, ktop, nbits, past_len):
    idx_refs = rest[0:PP]
    kp_refs = rest[PP:2 * PP]
    vp_refs = rest[2 * PP:3 * PP]
    o_ref = rest[3 * PP]
    keys_ref, lg_ref, p_ref, acc_ref, linv_ref, qi_ref, qg_ref, wcol_ref = rest[3 * PP + 1:]
    step = pl.program_id(1)
    G = DSA_HEADS // DSA_KV_HEADS
    GR = G * nq
    HR = DSA_HEADS * nq
    SW = PP * PAGE_SIZE
    scale = DSA_HD ** -0.5

    @pl.when(step == 0)
    def _():
        iq = iq_ref[...]
        dq = dq_ref[...]
        sm = sm_ref[...]
        qi_ref[...] = jnp.concatenate(
            [iq[:, h * IDX_DIM:(h + 1) * IDX_DIM] for h in range(IDX_HEADS)], axis=0).astype(BF16)
        qg_ref[...] = jnp.concatenate(
            [dq[:, h * DSA_HD:(h + 1) * DSA_HD] for h in range(DSA_HEADS)], axis=0).astype(BF16)
        wcol = jnp.concatenate(
            [sm[:, SM_IW + h:SM_IW + h + 1] for h in range(IDX_HEADS)], axis=0) * (IDX_HEADS ** -0.5)
        wcol_ref[...] = jnp.broadcast_to(wcol, (HR, LANES))

    def idx_scores(kib, transposed):
        s = _dot(qi_ref[...], kib) if transposed else _dot_nt(qi_ref[...], kib)
        s = jnp.maximum(s * (IDX_DIM ** -0.5), 0.0) * wcol_ref[...]
        return jnp.sum(s.reshape(IDX_HEADS, nq, s.shape[-1]), axis=0)

    def logits(kgs):
        return jnp.concatenate(
            [_dot_nt(qg_ref[g * GR:(g + 1) * GR, :], kgs[g].astype(BF16))
             for g in range(DSA_KV_HEADS)], axis=0) * scale

    page_rows = lambda ref, g: ref[0, pl.ds(g, PAGE_SIZE, stride=DSA_KV_HEADS), :]

    @pl.when(step < ns)
    def _():
        for pi in range(PP):
            lo, hi = pi * PAGE_SIZE, (pi + 1) * PAGE_SIZE
            keys_ref[step, :, lo:hi] = _to_key(idx_scores(idx_refs[pi][0].astype(BF16), True))
            lg_ref[step, :, lo:hi] = logits([page_rows(kp_refs[pi], g) for g in range(DSA_KV_HEADS)])

    @pl.when(step == ns - 1)
    def _():
        zpad = lambda z: jnp.concatenate(
            [z, jnp.zeros((PAGE_SIZE - nq, z.shape[1]), z.dtype)], axis=0)
        sm = sm_ref[...]
        n_idx = lax.broadcasted_iota(I32, (nq, PAGE_SIZE), 1)
        q_idx = lax.broadcasted_iota(I32, (nq, PAGE_SIZE), 0)
        sc_new = idx_scores(zpad(sm[:, SM_IK:SM_IK + IDX_DIM]).astype(BF16), False)
        keys_ref[ns, :, 0:PAGE_SIZE] = jnp.where(n_idx <= q_idx, _to_key(sc_new), INT_MIN)
        dkp = zpad(dk_ref[...])
        lg_ref[ns, :, 0:PAGE_SIZE] = logits([dkp[:, g * DSA_HD:(g + 1) * DSA_HD]
                                             for g in range(DSA_KV_HEADS)])

        lane = lax.broadcasted_iota(I32, (nq, SW), 1)

        def fold(fn, op):
            red, comb, init = _FOLD_OPS[op]
            acc = jnp.full((nq, PAGE_SIZE), init, I32)
            for s in range(ns):
                c = fn(keys_ref[s], s * SW + lane)
                for t in range(PP):
                    acc = comb(acc, c[:, t * PAGE_SIZE:(t + 1) * PAGE_SIZE])
            acc = comb(acc, fn(keys_ref[ns, :, 0:PAGE_SIZE], past_len + n_idx))
            return red(acc, axis=1, keepdims=True)

        n_valid = past_len + 1 + lax.broadcasted_iota(I32, (nq, 1), 0)
        tau, pc = _topk_search(fold, ktop, (nq, 1), nbits, n_valid)

        far = jnp.concatenate([jnp.full((nq, 1), rel_ref[BUCKET_FAR, h], F32)
                               for h in range(DSA_HEADS)], axis=0)
        near = [jnp.concatenate([_bias_from_buckets(bkt_ref[t], rel_ref, h)
                                 for h in range(DSA_HEADS)], axis=0) for t in range(2)]
        tile8 = lambda z: jnp.concatenate([z] * DSA_HEADS, axis=0)

        m = jnp.full((HR, 1), NEG_BIG, F32)
        for s in range(ns + 1):
            if s < ns:
                sel = _selected(keys_ref[s], s * SW + lane, tau, pc)
                bias = far
                lg = lg_ref[s] + tile8(jnp.where(sel, 0.0, NEG_BIG))
                if s == ns - 1:
                    lg = jnp.concatenate(
                        [lg[:, :SW - PAGE_SIZE] + far, lg[:, SW - PAGE_SIZE:] + near[0]], axis=1)
                else:
                    lg = lg + bias
                lg_ref[s] = lg
            else:
                sel = _selected(keys_ref[ns, :, 0:PAGE_SIZE], past_len + n_idx, tau, pc)
                lg = lg_ref[ns, :, 0:PAGE_SIZE] + tile8(jnp.where(sel, 0.0, NEG_BIG)) + near[1]
                lg_ref[ns, :, 0:PAGE_SIZE] = lg
            m = jnp.maximum(m, jnp.max(lg, axis=1, keepdims=True))

        l = jnp.zeros((HR, 1), F32)
        for s in range(ns):
            p = jnp.exp(lg_ref[s] - m)
            l = l + jnp.sum(p, axis=1, keepdims=True)
            p_ref[s] = p.astype(BF16)
        pn = jnp.exp(lg_ref[ns, :, 0:PAGE_SIZE] - m)
        l = l + jnp.sum(pn, axis=1, keepdims=True)
        linv_ref[...] = jnp.broadcast_to(1.0 / l, (HR, LANES))
        vn = zpad(dv_ref[...]).astype(BF16)
        pnb = pn.astype(BF16)
        for g in range(DSA_KV_HEADS):
            acc_ref[g * GR:(g + 1) * GR, :] = _dot(pnb[g * GR:(g + 1) * GR, :],
                                                   vn[:, g * DSA_HD:(g + 1) * DSA_HD])

    @pl.when(step >= ns)
    def _():
        s = step - ns
        for pi in range(PP):
            for g in range(DSA_KV_HEADS):
                acc_ref[g * GR:(g + 1) * GR, :] += _dot(
                    p_ref[s, g * GR:(g + 1) * GR, pi * PAGE_SIZE:(pi + 1) * PAGE_SIZE],
                    page_rows(vp_refs[pi], g).astype(BF16))

    @pl.when(step == 2 * ns - 1)
    def _():
        out = acc_ref[...] * linv_ref[...]
        for h in range(DSA_HEADS):
            o_ref[:, h * DSA_HD:(h + 1) * DSA_HD] = out[h * nq:(h + 1) * nq, :]


def _dsa_sample(page_table, rel_bias, dq, iq, sm, dk, dv, cache_idx, cache_k, cache_v, layer, n_pool):
    db, n_pages = page_table.shape
    nq = dq.shape[0] // db
    assert nq == SUBLANES and n_pages % PP == 0 and nq <= PAGE_SIZE
    ns = n_pages // PP
    past_len = n_pages * PAGE_SIZE
    ktop = min(TOPK_MAX, (past_len + nq) // 4)
    nbits = int(past_len + nq).bit_length()
    base = layer * n_pool
    q_idx = np.arange(nq)[:, None]
    lane = np.arange(PAGE_SIZE)[None, :]
    bkt = np.stack([_rel_bucket_np(PAGE_SIZE + q_idx - lane), _rel_bucket_np(q_idx - lane)])
    hr = DSA_HEADS * nq

    def page_spec(rows, width, phase, pi):
        if phase == 0:
            fn = lambda b, s, pt: (base + pt[b, jnp.minimum(s, ns - 1) * PP + pi], 0, 0)
        else:
            fn = lambda b, s, pt: (base + pt[b, jnp.maximum(s - ns, 0) * PP + pi], 0, 0)
        return pl.BlockSpec((1, rows, width), fn)

    tok = lambda wd: pl.BlockSpec((nq, wd), lambda b, s, pt: (b, 0))
    grid_spec = pltpu.PrefetchScalarGridSpec(
        num_scalar_prefetch=1,
        grid=(db, 2 * ns),
        in_specs=([pl.BlockSpec(memory_space=pltpu.SMEM),
                   pl.BlockSpec((2, nq, PAGE_SIZE), lambda b, s, pt: (0, 0, 0)),
                   tok(DSA_QW), tok(IDX_QW), tok(LANES), tok(DSA_KVW), tok(DSA_KVW)]
                  + [page_spec(IDX_DIM, PAGE_SIZE, 0, pi) for pi in range(PP)]
                  + [page_spec(PAGE_SIZE * DSA_KV_HEADS, DSA_HD, 0, pi) for pi in range(PP)]
                  + [page_spec(PAGE_SIZE * DSA_KV_HEADS, DSA_HD, 1, pi) for pi in range(PP)]),
        out_specs=tok(DSA_QW),
        scratch_shapes=[pltpu.VMEM((ns + 1, nq, PP * PAGE_SIZE), I32),
                        pltpu.VMEM((ns + 1, hr, PP * PAGE_SIZE), F32),
                        pltpu.VMEM((ns, hr, PP * PAGE_SIZE), BF16),
                        pltpu.VMEM((hr, DSA_HD), F32),
                        pltpu.VMEM((hr, LANES), F32),
                        pltpu.VMEM((hr, IDX_DIM), BF16),
                        pltpu.VMEM((hr, DSA_HD), BF16),
                        pltpu.VMEM((hr, LANES), F32)])
    return pl.pallas_call(
        functools.partial(_dsa_sample_kernel, nq=nq, ns=ns, ktop=ktop, nbits=nbits, past_len=past_len),
        grid_spec=grid_spec,
        out_shape=jax.ShapeDtypeStruct((db * nq, DSA_QW), F32),
        compiler_params=_params(("arbitrary", "arbitrary")),
        name="dsa_sample",
    )(page_table, rel_bias, jnp.asarray(bkt), dq, iq, sm, dk, dv,
      *([cache_idx] * PP), *([cache_k] * PP), *([cache_v] * PP))


def _merge_kernel(x_ref, oa_ref, ob_ref, ga_ref, gb_ref, wpa_ref, wpb_ref, wout_ref, gpost_ref, o_ref):
    pa = _dot(oa_ref[...].astype(BF16), wpa_ref[...])
    pb = _dot(ob_ref[...].astype(BF16), wpb_ref[...])
    m = _sigmoid(ga_ref[...]) * pa + _sigmoid(gb_ref[...]) * pb
    mo = _dot(m.astype(BF16), wout_ref[...])
    o_ref[...] = x_ref[...] + _rms(mo, gpost_ref[...])


def _merge(x, oa, ob, ga, gb, wpa, wpb, wout, gpost, tm):
    n = x.shape[0]
    assert n % tm == 0
    row = pl.BlockSpec((tm, D_MODEL), lambda i: (i, 0))
    wsp = pl.BlockSpec((D_MODEL, D_MODEL), lambda i: (0, 0))
    return pl.pallas_call(
        _merge_kernel,
        grid=(n // tm,),
        in_specs=[row, row, row, row, row, wsp, wsp, wsp, pl.BlockSpec((1, D_MODEL), lambda i: (0, 0))],
        out_specs=row,
        out_shape=jax.ShapeDtypeStruct((n, D_MODEL), F32),
        compiler_params=_params(("arbitrary",)),
        name="merge",
    )(x, oa, ob, ga, gb, wpa, wpb, wout, gpost)


FFN_TF = 1024
PREV_ROWS = 16


def _ffn_kernel(x_ref, xprev_ref, gpre_ref, gpost_ref, wg_ref, wu_ref, cw_ref, cb_ref, wd_ref, st_ref,
                o_ref, tail_ref, h2_ref, hp_ref, gext_ref, acc_ref,
                *, shift, nprev, tiles_per_seq, use_state, tail_off):
    i = pl.program_id(0)
    j = pl.program_id(1)
    tm = x_ref.shape[0]

    @pl.when(j == 0)
    def _():
        h2_ref[...] = _rms(x_ref[...], gpre_ref[...]).astype(BF16)
        hp_ref[...] = _rms(xprev_ref[...], gpre_ref[...]).astype(BF16)
        acc_ref[...] = jnp.zeros_like(acc_ref)

    h2 = h2_ref[...]
    g = _dot(h2, wg_ref[...])
    u = _dot(h2, wu_ref[...])
    if use_state:
        gprev = st_ref[...]
    else:
        gprev = _dot(hp_ref[...], wg_ref[...])
        gprev = jnp.where(i % tiles_per_seq == 0, 0.0, gprev)
    gext_ref[0:nprev, :] = gprev
    gext_ref[nprev:nprev + tm, :] = g
    cw = cw_ref[...]
    c = (cb_ref[...] + gext_ref[nprev - 2 * shift:nprev - 2 * shift + tm, :] * cw[0:1, :]
         + gext_ref[nprev - shift:nprev - shift + tm, :] * cw[1:2, :] + g * cw[2:3, :])
    gelu = 0.5 * c * (1.0 + jnp.tanh(math.sqrt(2.0 / math.pi) * (c + 0.044715 * (c * c * c))))
    acc_ref[...] += _dot((gelu * u).astype(BF16), wd_ref[...])
    tail_ref[...] = g[tail_off:tail_off + tail_ref.shape[0], :]

    @pl.when(j == pl.num_programs(1) - 1)
    def _():
        o_ref[...] = x_ref[...] + _rms(acc_ref[...], gpost_ref[...])


def _ffn(x, gpre, gpost, w_in_bf, conv_w, conv_b, w_down_bf, state, *, tm, shift, tiles_per_seq,
         use_state, tail_off, tail_rows):
    n = x.shape[0]
    assert n % tm == 0 and tm % PREV_ROWS == 0 and D_FF % FFN_TF == 0
    nprev = state.shape[0] if use_state else PREV_ROWS
    assert nprev >= 2 * shift and (not use_state or n == tm)
    nj = D_FF // FFN_TF
    pr = tm // PREV_ROWS
    return pl.pallas_call(
        functools.partial(_ffn_kernel, shift=shift, nprev=nprev, tiles_per_seq=tiles_per_seq,
                          use_state=use_state, tail_off=tail_off),
        grid=(n // tm, nj),
        in_specs=[pl.BlockSpec((tm, D_MODEL), lambda i, j: (i, 0)),
                  pl.BlockSpec((PREV_ROWS, D_MODEL), lambda i, j: (jnp.maximum(i * pr - 1, 0), 0)),
                  pl.BlockSpec((1, D_MODEL), lambda i, j: (0, 0)),
                  pl.BlockSpec((1, D_MODEL), lambda i, j: (0, 0)),
                  pl.BlockSpec((D_MODEL, FFN_TF), lambda i, j: (0, j)),
                  pl.BlockSpec((D_MODEL, FFN_TF), lambda i, j: (0, j + nj)),
                  pl.BlockSpec((CONV_W, FFN_TF), lambda i, j: (0, j)),
                  pl.BlockSpec((1, FFN_TF), lambda i, j: (0, j)),
                  pl.BlockSpec((FFN_TF, D_MODEL), lambda i, j: (j, 0)),
                  pl.BlockSpec((state.shape[0], FFN_TF), lambda i, j: (0, j))],
        out_specs=[pl.BlockSpec((tm, D_MODEL), lambda i, j: (i, 0)),
                   pl.BlockSpec((tail_rows, FFN_TF), lambda i, j: (i, j))],
        out_shape=[jax.ShapeDtypeStruct((n, D_MODEL), F32),
                   jax.ShapeDtypeStruct((n // tm * tail_rows, D_FF), F32)],
        scratch_shapes=[pltpu.VMEM((tm, D_MODEL), BF16),
                        pltpu.VMEM((PREV_ROWS, D_MODEL), BF16),
                        pltpu.VMEM((nprev + tm, FFN_TF), F32),
                        pltpu.VMEM((tm, D_MODEL), F32)],
        compiler_params=_params(("arbitrary", "arbitrary")),
        name="ffn",
    )(x, x, gpre, gpost, w_in_bf, w_in_bf, conv_w, conv_b, w_down_bf, state)


def _round_up(a, m):
    return -(-a // m) * m


def kernel(x_prompt, x_sample, cache_k, cache_v, cache_idx_k, state_gla, state_conv, page_table, meta_tokens, rel_bias, norm_mix_pre, norm_mix_post, norm_ffn_pre, norm_ffn_post, w_in, w_a2, b_a, gla_norm, w_pa, w_pb, w_out, w_ffn_in, conv_w, conv_b, w_ffn_down):
    bsz, seq = x_prompt.shape[:2]
    db, nq = x_sample.shape[:2]
    depth = w_in.shape[0]
    n_pool = cache_k.shape[1]
    t_real = seq + N_META
    tp = _round_up(t_real, QB)
    nb = tp // QB
    row2 = lambda a: a.reshape(1, -1)
    seg_names = [n for n, _ in SEGS] + ["bcum"]

    w_in_r = [_relayout_w_in(w_in[l]) for l in range(depth)]
    w_pa_b, w_pb_b, w_out_b = w_pa.astype(BF16), w_pb.astype(BF16), w_out.astype(BF16)
    w_ffn_in_b, w_ffn_down_b = w_ffn_in.astype(BF16), w_ffn_down.astype(BF16)

    ck = cache_k.reshape(depth * n_pool, PAGE_SIZE * DSA_KV_HEADS, DSA_HD)
    cv = cache_v.reshape(depth * n_pool, PAGE_SIZE * DSA_KV_HEADS, DSA_HD)
    ci = cache_idx_k.transpose(0, 1, 3, 2).reshape(depth * n_pool, IDX_DIM, PAGE_SIZE)

    xp = jnp.concatenate([jnp.broadcast_to(meta_tokens[None].astype(x_prompt.dtype), (bsz, N_META, D_MODEL)),
                          x_prompt, jnp.zeros((bsz, tp - t_real, D_MODEL), x_prompt.dtype)], axis=1)
    xp = xp.reshape(bsz * tp, D_MODEL)
    ffn_tm = tp // 4 if (tp // 4) % PREV_ROWS == 0 else tp
    tiles_per_seq = tp // ffn_tm
    tail_pos = (t_real - (CONV_W - 1)) % ffn_tm
    tail_off = tail_pos // SUBLANES * SUBLANES
    assert tail_pos - tail_off + (CONV_W - 1) <= SUBLANES
    pk, pv, pik, pgla, pconv = [], [], [], [], []
    zero_state = jnp.zeros((bsz, GLA_HEADS, GLA_DK, GLA_DV), F32)
    zero_conv = jnp.zeros((PREV_ROWS, D_FF), F32)
    for l in range(depth):
        z = dict(zip(seg_names, _proj(xp, row2(norm_mix_pre[l]), w_in_r[l], w_a2[l], row2(b_a[l]),
                                      2 * CHUNK, CHUNK, tp, t_real)))
        r3 = lambda a: a.reshape(bsz, tp, a.shape[-1])
        o_a, s_fin = _gla(r3(z["gq"]), r3(z["gk"]), r3(z["gv"]), r3(z["gr"]), r3(z["bcum"]),
                          row2(gla_norm[l]), zero_state, 0, t_real)
        dk3, dv3, sm3 = r3(z["dk"]), r3(z["dv"]), r3(z["sm"])
        o_b = _dsa_prompt(rel_bias, r3(z["dq"]), r3(z["iq"]), sm3, dk3, dv3, t_real)
        xm = _merge(xp, o_a.reshape(bsz * tp, GLA_VW), o_b.reshape(bsz * tp, DSA_QW), z["ga"], z["gb"],
                    w_pa_b[l], w_pb_b[l], w_out_b[l], row2(norm_mix_post[l]), 512 if (bsz * tp) % 512 == 0 else CHUNK)
        xp, tail = _ffn(xm, row2(norm_ffn_pre[l]), row2(norm_ffn_post[l]), w_ffn_in_b[l], conv_w[l],
                        row2(conv_b[l]), w_ffn_down_b[l], zero_conv, tm=ffn_tm, shift=1,
                        tiles_per_seq=tiles_per_seq, use_state=False, tail_off=tail_off, tail_rows=SUBLANES)
        pk.append(dk3[:, :t_real].reshape(bsz, t_real, DSA_KV_HEADS, DSA_HD))
        pv.append(dv3[:, :t_real].reshape(bsz, t_real, DSA_KV_HEADS, DSA_HD))
        pik.append(sm3[:, :t_real, SM_IK:SM_IK + IDX_DIM])
        pgla.append(s_fin)
        tail = tail.reshape(bsz, tiles_per_seq, SUBLANES, D_FF)[:, (t_real - 1) // ffn_tm]
        pconv.append(tail[:, tail_pos - tail_off:tail_pos - tail_off + CONV_W - 1])
    y_prompt = xp.reshape(bsz, tp, D_MODEL)[:, N_META:t_real]

    ns_rows = db * nq
    xs = x_sample.reshape(ns_rows, D_MODEL)
    sk, sv, sik, sgla, sconv = [], [], [], [], []
    state_all = state_gla.reshape(depth * db, GLA_HEADS, GLA_DK, GLA_DV)
    for l in range(depth):
        z = dict(zip(seg_names, _proj(xs, row2(norm_mix_pre[l]), w_in_r[l], w_a2[l], row2(b_a[l]),
                                      ns_rows, nq, nq, nq)))
        r3 = lambda a: a.reshape(db, nq, a.shape[-1])
        o_a, s_fin = _gla(r3(z["gq"]), r3(z["gk"]), r3(z["gv"]), r3(z["gr"]), r3(z["bcum"]),
                          row2(gla_norm[l]), state_all, l * db, nq)
        o_b = _dsa_sample(page_table, rel_bias, z["dq"], z["iq"], z["sm"], z["dk"], z["dv"],
                          ci, ck, cv, l, n_pool)
        xm = _merge(xs, o_a.reshape(ns_rows, GLA_VW), o_b, z["ga"], z["gb"],
                    w_pa_b[l], w_pb_b[l], w_out_b[l], row2(norm_mix_post[l]), ns_rows)
        xm_t = xm.reshape(db, nq, D_MODEL).transpose(1, 0, 2).reshape(ns_rows, D_MODEL)
        st = state_conv[l].transpose(1, 0, 2).reshape((CONV_W - 1) * db, D_FF)
        xo_t, tail = _ffn(xm_t, row2(norm_ffn_pre[l]), row2(norm_ffn_post[l]), w_ffn_in_b[l], conv_w[l],
                          row2(conv_b[l]), w_ffn_down_b[l], st, tm=ns_rows, shift=db, tiles_per_seq=1,
                          use_state=True, tail_off=(nq - (CONV_W - 1)) * db, tail_rows=(CONV_W - 1) * db)
        xs = xo_t.reshape(nq, db, D_MODEL).transpose(1, 0, 2).reshape(ns_rows, D_MODEL)
        sk.append(z["dk"].reshape(db, nq, DSA_KV_HEADS, DSA_HD))
        sv.append(z["dv"].reshape(db, nq, DSA_KV_HEADS, DSA_HD))
        sik.append(z["sm"][:, SM_IK:SM_IK + IDX_DIM].reshape(db, nq, IDX_DIM))
        sgla.append(s_fin)
        sconv.append(tail.reshape(CONV_W - 1, db, D_FF).transpose(1, 0, 2))
    y_sample = xs.reshape(db, nq, D_MODEL)

    return (y_prompt, y_sample,
            jnp.stack(pk), jnp.stack(pv), jnp.stack(pik), jnp.stack(pgla), jnp.stack(pconv),
            jnp.stack(sk), jnp.stack(sv), jnp.stack(sik), jnp.stack(sgla), jnp.stack(sconv))
```

```python
import functools
import math

import jax
import jax.numpy as jnp
import numpy as np
from jax import lax
from jax.experimental import pallas as pl
from jax.experimental.pallas import tpu as pltpu

F32 = jnp.float32
BF16 = jnp.bfloat16
I32 = jnp.int32

D_MODEL = 1024
N_META = 16
GLA_HEADS = 4
GLA_DK = 128
GLA_DV = 256
GLA_RANK = 16
GLA_TAU = 16.0
DSA_HEADS = 8
DSA_KV_HEADS = 2
DSA_HD = 128
IDX_HEADS = 8
IDX_DIM = 64
TOPK_MAX = 256
REL_BUCKETS = 32
REL_MAX_DIST = 128
D_FF = 4096
CONV_W = 3
EPS = 1e-6
PAGE_SIZE = 128

GLA_KW = GLA_HEADS * GLA_DK
GLA_VW = GLA_HEADS * GLA_DV
DSA_QW = DSA_HEADS * DSA_HD
DSA_KVW = DSA_KV_HEADS * DSA_HD
IDX_QW = IDX_HEADS * IDX_DIM

LANES = 128
SUBLANES = 8
CHUNK = 128
KB = 256
QB = 128
COUNT_BLOCKS = 2
KEYS_PAD_BLOCKS = 3
LOG2E = math.log2(math.e)
VMEM_LIMIT = 56 * 1024 * 1024

INT_MIN = -(2 ** 31)
NEG_BIG = -1e30

SEGS = (("gq", GLA_KW), ("gk", GLA_KW), ("gv", GLA_VW), ("gr", GLA_VW), ("dq", DSA_QW),
        ("dk", DSA_KVW), ("dv", DSA_KVW), ("iq", IDX_QW), ("ga", D_MODEL), ("gb", D_MODEL),
        ("sm", LANES))
SM_IK = 0
SM_GA1 = IDX_DIM
SM_IW = IDX_DIM + GLA_RANK
SEG_OFFS = tuple(int(v) for v in np.cumsum([0] + [w for _, w in SEGS]))
PW = SEG_OFFS[-1]


def _relayout_w_in(w):
    o = np.cumsum([0, GLA_KW, GLA_KW, GLA_VW, GLA_VW, GLA_RANK, DSA_QW, DSA_KVW, DSA_KVW,
                   IDX_QW, IDX_HEADS, IDX_DIM, D_MODEL, D_MODEL]).tolist()
    c = lambda i: w[:, o[i]:o[i + 1]]
    pad = jnp.zeros((w.shape[0], LANES - IDX_DIM - GLA_RANK - IDX_HEADS), w.dtype)
    cols = [c(0), c(1), c(2), c(3), c(5), c(6), c(7), c(8), c(11), c(12), c(10), c(4), c(9), pad]
    return jnp.concatenate(cols, axis=1).astype(BF16)


def _rms(x, g):
    return x * lax.rsqrt(jnp.mean(x * x, axis=-1, keepdims=True) + EPS) * g


def _sigmoid(x):
    return 1.0 / (1.0 + jnp.exp(-x))


def _dot(a, b):
    return jnp.dot(a, b, preferred_element_type=F32)


def _dot_nt(a, b):
    return lax.dot_general(a, b, (((1,), (1,)), ((), ())), preferred_element_type=F32)


def _params(sem):
    return pltpu.CompilerParams(dimension_semantics=sem, vmem_limit_bytes=VMEM_LIMIT)


def _split3(z):
    hi = z.astype(BF16)
    r = z - hi.astype(F32)
    mid = r.astype(BF16)
    return hi, mid, (r - mid.astype(F32)).astype(BF16)


def _proj_kernel(x_ref, g_ref, w_ref, wa2_ref, ba_ref, *out_refs, chunk, seq_rows, t_valid):
    tm = x_ref.shape[0]
    hb = _rms(x_ref[...], g_ref[...]).astype(BF16)
    segment = lambda s: _dot(hb, w_ref[:, SEG_OFFS[s]:SEG_OFFS[s + 1]])
    assert SEGS[-1][0] == "sm"
    seg = segment(len(SEGS) - 1)
    out_refs[len(SEGS) - 1][...] = seg
    ga1 = seg[:, SM_GA1:SM_GA1 + GLA_RANK]
    x = jnp.dot(ga1, wa2_ref[...], precision=lax.Precision.HIGHEST,
                preferred_element_type=F32) + ba_ref[...]
    la = (jnp.minimum(x, 0.0) - jnp.log1p(jnp.exp(-jnp.abs(x)))) * (1.0 / GLA_TAU)
    row = pl.program_id(0) * tm + lax.broadcasted_iota(I32, (tm, 1), 0)
    la = jnp.where(row % seq_rows < t_valid, la, 0.0)
    ti = lax.broadcasted_iota(I32, (tm, tm), 0)
    si = lax.broadcasted_iota(I32, (tm, tm), 1)
    tri = jnp.where((ti // chunk == si // chunk) & (ti >= si), 1.0, 0.0).astype(BF16)
    out_refs[-1][...] = sum(_dot(tri, piece) for piece in _split3(la))
    for s in range(len(SEGS) - 1):
        out_refs[s][...] = segment(s)


def _proj(x, g, w, wa2, ba, tm, chunk, seq_rows, t_valid):
    n = x.shape[0]
    assert n % tm == 0 and tm % chunk == 0 and seq_rows % chunk == 0
    widths = [wd for _, wd in SEGS] + [GLA_KW]
    full = lambda a: pl.BlockSpec(a.shape, lambda i: (0, 0))
    return pl.pallas_call(
        functools.partial(_proj_kernel, chunk=chunk, seq_rows=seq_rows, t_valid=t_valid),
        grid=(n // tm,),
        in_specs=[pl.BlockSpec((tm, D_MODEL), lambda i: (i, 0)), full(g), full(w), full(wa2), full(ba)],
        out_specs=[pl.BlockSpec((tm, wd), lambda i: (i, 0)) for wd in widths],
        out_shape=[jax.ShapeDtypeStruct((n, wd), F32) for wd in widths],
        compiler_params=_params(("arbitrary",)),
        name="proj",
    )(x, g, w, wa2, ba)


def _gla_kernel(q_ref, k_ref, v_ref, gr_ref, bc_ref, gain_ref, s0_ref,
                o_ref, sfin_ref, s_ref, *, t_valid, rows):
    C = CHUNK
    c = pl.program_id(2)

    @pl.when(c == 0)
    def _():
        s_ref[...] = s0_ref[0, 0]

    def padrows(z):
        if rows == C:
            return z
        return jnp.concatenate([z, jnp.zeros((C - rows, z.shape[1]), z.dtype)], axis=0)

    rowi = lax.broadcasted_iota(I32, (C, 1), 0)
    live = c * C + rowi < t_valid
    q = padrows(q_ref[0]) * (GLA_DK ** -0.5)
    k = jnp.where(live, padrows(k_ref[0]), 0.0)
    v = padrows(v_ref[0])
    ti = lax.broadcasted_iota(I32, (C, C), 0)
    si = lax.broadcasted_iota(I32, (C, C), 1)
    b = bc_ref[0]
    if rows < C:
        b = jnp.concatenate([b, jnp.broadcast_to(b[rows - 1:rows, :], (C - rows, GLA_DK))], axis=0)

    att = None if rows > SUBLANES else jnp.zeros((C, C), F32)
    m = C // 2
    while m >= SUBLANES and rows > SUBLANES:
        pieces = []
        for p in range(C // (2 * m)):
            r = p * 2 * m + m - 1
            pieces.append(jnp.broadcast_to(b[r:r + 1, :], (2 * m, GLA_DK)))
        bound = pieces[0] if len(pieces) == 1 else jnp.concatenate(pieces, axis=0)
        upper = ((rowi // m) % 2) == 1
        e = jnp.exp(-jnp.abs(b - bound))
        qm = jnp.where(upper, q * e, 0.0)
        km = jnp.where(upper, 0.0, k * e)
        a = _dot_nt(qm.astype(BF16), km.astype(BF16))
        if 2 * m < C:
            a = jnp.where((ti // (2 * m)) == (si // (2 * m)), a, 0.0)
        att = a if att is None else att + a
        m //= 2

    tm8 = rowi % SUBLANES
    diag = ti - si
    for delta in range(SUBLANES):
        kd = k if delta == 0 else pltpu.roll(k, delta, 0)
        bd = b if delta == 0 else pltpu.roll(b, delta, 0)
        w = q * kd * jnp.exp(jnp.minimum(b - bd, 0.0))
        a = jnp.sum(w, axis=1, keepdims=True)
        att = jnp.where((diag == delta) & (tm8 >= delta), a, att)

    s = s_ref[...]
    vb = v.astype(BF16)
    inter = _dot((q * jnp.exp(b)).astype(BF16), s.astype(BF16))
    o = inter + _dot(att.astype(BF16), vb)

    b_last_row = b[C - 1:C, :]
    b_last_col = b.T[:, C - 1:C]
    kdec = k * jnp.exp(b_last_row - b)
    s_new = s * jnp.exp(b_last_col) + _dot(kdec.T.astype(BF16), vb)
    s_ref[...] = s_new

    on = o * lax.rsqrt(jnp.mean(o * o, axis=-1, keepdims=True) + EPS) * gain_ref[...]
    gr = padrows(gr_ref[0])
    res = on * (gr * _sigmoid(gr))
    o_ref[0] = res[:rows].astype(o_ref.dtype)

    @pl.when(c == pl.num_programs(2) - 1)
    def _():
        sfin_ref[0, 0] = s_new


def _gla(gq, gk, gv, gr, bcum, gain, s0, s0_first, t_valid):
    bsz, tp = gq.shape[:2]
    rows = min(tp, CHUNK)
    assert tp % rows == 0
    nc = tp // rows
    tok = lambda wd: pl.BlockSpec((1, rows, wd), lambda b, h, c: (b, c, h))
    return pl.pallas_call(
        functools.partial(_gla_kernel, t_valid=t_valid, rows=rows),
        grid=(bsz, GLA_HEADS, nc),
        in_specs=[tok(GLA_DK), tok(GLA_DK), tok(GLA_DV), tok(GLA_DV), tok(GLA_DK),
                  pl.BlockSpec((1, GLA_DV), lambda b, h, c: (0, 0)),
                  pl.BlockSpec((1, 1, GLA_DK, GLA_DV), lambda b, h, c: (s0_first + b, h, 0, 0))],
        out_specs=[pl.BlockSpec((1, rows, GLA_DV), lambda b, h, c: (b, c, h)),
                   pl.BlockSpec((1, 1, GLA_DK, GLA_DV), lambda b, h, c: (b, h, 0, 0))],
        out_shape=[jax.ShapeDtypeStruct((bsz, tp, GLA_VW), F32),
                   jax.ShapeDtypeStruct((bsz, GLA_HEADS, GLA_DK, GLA_DV), F32)],
        scratch_shapes=[pltpu.VMEM((GLA_DK, GLA_DV), F32)],
        compiler_params=_params(("arbitrary", "arbitrary", "arbitrary")),
        name="gla",
    )(gq, gk, gv, gr, bcum, gain, s0)


def _rel_bucket_np(dist):
    max_exact = REL_BUCKETS // 2
    d = np.maximum(dist, 0)
    large = max_exact + (np.log(np.maximum(d, 1).astype(np.float64) / max_exact)
                         / math.log(REL_MAX_DIST / max_exact) * (REL_BUCKETS - max_exact)).astype(np.int64)
    large = np.minimum(large, REL_BUCKETS - 1)
    return np.where(d < max_exact, d, large).astype(np.int32)


BUCKET_FAR = int(_rel_bucket_np(np.array([REL_MAX_DIST]))[0])
assert (_rel_bucket_np(np.arange(REL_MAX_DIST, 4 * 8192)) == BUCKET_FAR).all()
assert REL_MAX_DIST <= PAGE_SIZE and REL_MAX_DIST <= KB


def _to_key(x):
    bits = lax.bitcast_convert_type(x + 0.0, I32)
    return jnp.where(bits < 0, bits ^ 0x7FFFFFFF, bits)


def _bias_from_buckets(bkt, rel_ref, h):
    acc = jnp.zeros(bkt.shape, F32)
    for bb in range(REL_BUCKETS):
        acc = jnp.where(bkt == bb, rel_ref[bb, h], acc)
    return acc


_FOLD_OPS = {"sum": (jnp.sum, jnp.add, 0)}
BITS_PER_TRIP = 4


def _topk_search(fold, ktop, shape, nbits, n_valid):
    count = lambda pred: fold(lambda kt, pos: jnp.where(pred(kt, pos), 1, 0).astype(I32), "sum")

    def pending(fin):
        return jnp.max(jnp.where(fin > 0, 0.0, 1.0)) > 0.0

    cnt0 = count(lambda kt, pos: kt >= 0)
    tau = jnp.where(cnt0 >= ktop, jnp.zeros(shape, I32), jnp.full(shape, INT_MIN, I32))
    cge = jnp.where(cnt0 >= ktop, cnt0, n_valid + jnp.zeros(shape, I32))

    def one_bit(it, st):
        tau, cge = st
        cand = tau | lax.shift_left(jnp.int32(1), 30 - it)
        cnt = count(lambda kt, pos: kt >= cand)
        up = cnt >= ktop
        return jnp.where(up, cand, tau), jnp.where(up, cnt, cge)

    lead = 31 % BITS_PER_TRIP

    def bits(trip, st):
        for u in range(BITS_PER_TRIP):
            st = one_bit(lead + trip * BITS_PER_TRIP + u, st)
        return st

    st = (tau, cge)
    for it in range(lead):
        st = one_bit(it, st)
    tau, cge = lax.fori_loop(0, 31 // BITS_PER_TRIP, bits, st)
    few = n_valid <= ktop
    tau = jnp.where(few, INT_MIN, tau)
    fin = jnp.where(few | (cge == ktop), 1, 0)

    def tie_search():
        need = ktop - count(lambda kt, pos: kt > tau)

        def pos_body(it, pc):
            cand = pc | lax.shift_left(jnp.int32(1), nbits - 1 - it)
            cnt = count(lambda kt, pos: (kt == tau) & (pos < cand))
            return jnp.where(cnt <= need, cand, pc)

        return lax.fori_loop(0, nbits, pos_body, jnp.zeros(shape, I32))

    pc = lax.cond(pending(fin), tie_search, lambda: jnp.full(shape, (1 << nbits) - 1, I32))
    return tau, pc


def _selected(kt, pos, tau, pc):
    return (kt > tau) | ((kt == tau) & (pos < pc) & (tau > INT_MIN))


def _dsa_prompt_kernel(rel_ref, bkt_ref, dq_ref, iq_ref, sm_ref, ik_ref, k_ref, vt_ref, o_ref,
                       keys_ref, bias_ref, qt_ref, qit_ref, m_ref, acc_ref,
                       sa_ref, sb_ref, la_ref, lb_ref, *, ktop, nbits):
    bi = pl.program_id(0)
    i = pl.program_id(1)
    G = DSA_HEADS // DSA_KV_HEADS
    hcols = lambda hh: slice(hh * QB, (hh + 1) * QB)

    @pl.when((bi == 0) & (i == 0))
    def _():
        for h in range(DSA_HEADS):
            g, hh = divmod(h, G)
            for t in range(2):
                bias_ref[t, g, :, hcols(hh)] = _bias_from_buckets(bkt_ref[t], rel_ref, h) * LOG2E
            bias_ref[2, g, :, hcols(hh)] = jnp.full((QB, QB), rel_ref[BUCKET_FAR, h] * LOG2E, F32)

    dq = dq_ref[0] * (DSA_HD ** -0.5 * LOG2E)
    for h in range(DSA_HEADS):
        g, hh = divmod(h, G)
        qt_ref[g, :, hcols(hh)] = dq[:, h * DSA_HD:(h + 1) * DSA_HD].T.astype(BF16)
    iq = iq_ref[0]
    per = LANES // IDX_DIM
    for p in range(IDX_QW // LANES):
        t = iq[:, p * LANES:(p + 1) * LANES].T
        for r in range(per):
            qit_ref[:, hcols(p * per + r)] = t[r * IDX_DIM:(r + 1) * IDX_DIM].astype(BF16)
    smt = sm_ref[0].T
    wrow = jnp.concatenate([smt[SM_IW + h:SM_IW + h + 1, :] for h in range(IDX_HEADS)],
                           axis=1) * (IDX_HEADS ** -0.5 * IDX_DIM ** -0.5)
    assert math.log2(IDX_DIM ** -0.5).is_integer()

    s_rel = lax.broadcasted_iota(I32, (KB, QB), 0)
    q_rel = lax.broadcasted_iota(I32, (KB, QB), 1)
    s_rel2 = lax.broadcasted_iota(I32, (2 * KB, QB), 0)

    nkb = (i * QB + QB + KB - 1) // KB
    npair = (nkb + 1) // 2
    last_blk = vt_ref.shape[2] - 1
    data_off = lambda j: pl.multiple_of(jnp.minimum(j, last_blk) * KB, KB)

    def idx_dot(j):
        return _dot(ik_ref[0, pl.ds(data_off(j), KB), :], qit_ref[...])

    def score_block(j, src_ref, dst_ref):
        dst_ref[...] = idx_dot(j + 1)
        s = jnp.maximum(src_ref[...], 0.0) * wrow
        acc = s[:, hcols(0)]
        for h in range(1, IDX_HEADS):
            acc = acc + s[:, hcols(h)]
        valid = (j * KB + s_rel) <= (i * QB + q_rel)
        keys_ref[pl.ds(pl.multiple_of(j * KB, KB), KB), :] = jnp.where(valid, _to_key(acc), INT_MIN)

    sa_ref[...] = idx_dot(0)

    def scores(t, carry):
        score_block(2 * t, sa_ref, sb_ref)
        score_block(2 * t + 1, sb_ref, sa_ref)
        return carry

    lax.fori_loop(0, npair, scores, 0)

    span = 2 * KB
    keys_ref[pl.ds(pl.multiple_of(npair * span, span), span), :] = jnp.full((span, QB), INT_MIN, I32)
    per_trip = COUNT_BLOCKS // 2
    ntrip = (npair + per_trip - 1) // per_trip

    def fold(fn, op):
        red, comb, init = _FOLD_OPS[op]

        def body(t, acc):
            for u in range(per_trip):
                off = (t * per_trip + u) * span
                kt = keys_ref[pl.ds(pl.multiple_of(off, span), span), :]
                c = fn(kt, off + s_rel2)
                acc = comb(acc, red(c.reshape(span // SUBLANES, SUBLANES, QB), axis=0))
            return acc
        acc = lax.fori_loop(0, ntrip, body, jnp.full((SUBLANES, QB), init, I32))
        return red(acc, axis=0, keepdims=True)

    n_valid = i * QB + lax.broadcasted_iota(I32, (1, QB), 1) + 1
    tau, pc = _topk_search(fold, ktop, (1, QB), nbits, n_valid)

    m_ref[...] = jnp.full(m_ref.shape, NEG_BIG, F32)
    acc_ref[...] = jnp.zeros(acc_ref.shape, F32)
    sub = KB // QB

    def qk(j, dst_ref):
        for g in range(DSA_KV_HEADS):
            kj = k_ref[0, pl.ds(data_off(j), KB), g * DSA_HD:(g + 1) * DSA_HD]
            dst_ref[g] = _dot(kj, qt_ref[g])

    def attend_block(j, src_ref, dst_ref):
        qk(j + 1, dst_ref)
        sel = _selected(keys_ref[pl.ds(pl.multiple_of(j * KB, KB), KB), :], j * KB + s_rel, tau, pc)
        madd = jnp.where(sel, 0.0, NEG_BIG)
        madd = jnp.concatenate([madd] * G, axis=1)
        alphas, pvs = [], []
        for g in range(DSA_KV_HEADS):
            bias = jnp.concatenate(
                [bias_ref[jnp.clip(i - (j * sub + u), 0, 2), g] for u in range(sub)], axis=0)
            lg = src_ref[g] + bias + madd
            m = m_ref[g]
            mn = jnp.maximum(m, jnp.max(lg, axis=0, keepdims=True))
            alpha = jnp.exp2(m - mn)
            p = jnp.exp2(lg - mn)
            m_ref[g] = mn
            alphas.append(alpha)
            pvs.append(_dot(vt_ref[0, g, jnp.minimum(j, last_blk)], p.astype(BF16)))
        for g in range(DSA_KV_HEADS):
            acc_ref[g] = alphas[g] * acc_ref[g] + pvs[g]

    qk(0, la_ref)

    def attend(t, carry):
        attend_block(2 * t, la_ref, lb_ref)
        attend_block(2 * t + 1, lb_ref, la_ref)
        return carry

    lax.fori_loop(0, npair, attend, 0)
    for h in range(DSA_HEADS):
        g, hh = divmod(h, G)
        out = acc_ref[g, 0:DSA_HD, hcols(hh)] / acc_ref[g, DSA_HD:DSA_HD + 1, hcols(hh)]
        o_ref[0, :, h * DSA_HD:(h + 1) * DSA_HD] = out.T.astype(o_ref.dtype)


def _dsa_prompt(rel_bias, dq, iq, sm, dk, dv, t_real):
    bsz, tp = dq.shape[:2]
    nb = tp // QB
    tk = _round_up(tp, KB)
    nkb = tk // KB
    padk = lambda a: jnp.pad(a.astype(BF16), ((0, 0), (0, tk - tp), (0, 0)))
    ik_bf = padk(sm[:, :, SM_IK:SM_IK + IDX_DIM])
    k_bf = padk(dk)
    vt_bf = padk(dv).reshape(bsz, nkb, KB, DSA_KV_HEADS, DSA_HD).transpose(0, 3, 1, 4, 2)
    vt_bf = jnp.concatenate([vt_bf, jnp.ones(vt_bf.shape[:3] + (SUBLANES, KB), BF16)], axis=3)
    vrows = DSA_HD + SUBLANES
    ktop = min(TOPK_MAX, t_real // 4)
    nbits = int(tk).bit_length()
    gw = DSA_HEADS // DSA_KV_HEADS * QB
    s_rel = np.arange(QB)[:, None]
    q_rel = np.arange(QB)[None, :]
    bkt = np.stack([_rel_bucket_np(q_rel - s_rel), _rel_bucket_np(QB + q_rel - s_rel)])
    return pl.pallas_call(
        functools.partial(_dsa_prompt_kernel, ktop=ktop, nbits=nbits),
        grid=(bsz, nb),
        in_specs=[pl.BlockSpec(memory_space=pltpu.SMEM),
                  pl.BlockSpec((2, QB, QB), lambda b, i: (0, 0, 0)),
                  pl.BlockSpec((1, QB, DSA_QW), lambda b, i: (b, i, 0)),
                  pl.BlockSpec((1, QB, IDX_QW), lambda b, i: (b, i, 0)),
                  pl.BlockSpec((1, QB, LANES), lambda b, i: (b, i, 0)),
                  pl.BlockSpec((1, tk, IDX_DIM), lambda b, i: (b, 0, 0)),
                  pl.BlockSpec((1, tk, DSA_KVW), lambda b, i: (b, 0, 0)),
                  pl.BlockSpec((1, DSA_KV_HEADS, nkb, vrows, KB), lambda b, i: (b, 0, 0, 0, 0))],
        out_specs=pl.BlockSpec((1, QB, DSA_QW), lambda b, i: (b, i, 0)),
        out_shape=jax.ShapeDtypeStruct((bsz, tp, DSA_QW), F32),
        scratch_shapes=[pltpu.VMEM((tk + KEYS_PAD_BLOCKS * KB, QB), I32),
                        pltpu.VMEM((3, DSA_KV_HEADS, QB, gw), F32),
                        pltpu.VMEM((DSA_KV_HEADS, DSA_HD, gw), BF16),
                        pltpu.VMEM((IDX_DIM, IDX_HEADS * QB), BF16),
                        pltpu.VMEM((DSA_KV_HEADS, 1, gw), F32),
                        pltpu.VMEM((DSA_KV_HEADS, vrows, gw), F32),
                        pltpu.VMEM((KB, IDX_HEADS * QB), F32),
                        pltpu.VMEM((KB, IDX_HEADS * QB), F32),
                        pltpu.VMEM((DSA_KV_HEADS, KB, gw), F32),
                        pltpu.VMEM((DSA_KV_HEADS, KB, gw), F32)],
        compiler_params=_params(("arbitrary", "arbitrary")),
        name="dsa_prompt",
    )(rel_bias, jnp.asarray(bkt), dq, iq, sm, ik_bf, k_bf, vt_bf)


PP = 16


def _dsa_sample_kernel(pt_ref, rel_ref, bkt_ref, dq_ref, iq_ref, sm_ref, dk_ref, dv_ref, *rest,
                       nq, ns, ktop, nbits, past_len):
    idx_refs = rest[0:PP]
    kp_refs = rest[PP:2 * PP]
    vp_refs = rest[2 * PP:3 * PP]
    o_ref = rest[3 * PP]
    keys_ref, lg_ref, p_ref, acc_ref, linv_ref, qi_ref, qg_ref, wcol_ref = rest[3 * PP + 1:]
    step = pl.program_id(1)
    G = DSA_HEADS // DSA_KV_HEADS
    GR = G * nq
    HR = DSA_HEADS * nq
    SW = PP * PAGE_SIZE
    scale = DSA_HD ** -0.5

    @pl.when(step == 0)
    def _():
        iq = iq_ref[...]
        dq = dq_ref[...]
        sm = sm_ref[...]
        qi_ref[...] = jnp.concatenate(
            [iq[:, h * IDX_DIM:(h + 1) * IDX_DIM] for h in range(IDX_HEADS)], axis=0).astype(BF16)
        qg_ref[...] = jnp.concatenate(
            [dq[:, h * DSA_HD:(h + 1) * DSA_HD] for h in range(DSA_HEADS)], axis=0).astype(BF16)
        wcol = jnp.concatenate(
            [sm[:, SM_IW + h:SM_IW + h + 1] for h in range(IDX_HEADS)], axis=0) * (IDX_HEADS ** -0.5)
        wcol_ref[...] = jnp.broadcast_to(wcol, (HR, LANES))

    def idx_scores(kib, transposed):
        s = _dot(qi_ref[...], kib) if transposed else _dot_nt(qi_ref[...], kib)
        s = jnp.maximum(s * (IDX_DIM ** -0.5), 0.0) * wcol_ref[...]
        return jnp.sum(s.reshape(IDX_HEADS, nq, s.shape[-1]), axis=0)

    def logits(kgs):
        return jnp.concatenate(
            [_dot_nt(qg_ref[g * GR:(g + 1) * GR, :], kgs[g].astype(BF16))
             for g in range(DSA_KV_HEADS)], axis=0) * scale

    page_rows = lambda ref, g: ref[0, pl.ds(g, PAGE_SIZE, stride=DSA_KV_HEADS), :]

    @pl.when(step < ns)
    def _():
        for pi in range(PP):
            lo, hi = pi * PAGE_SIZE, (pi + 1) * PAGE_SIZE
            keys_ref[step, :, lo:hi] = _to_key(idx_scores(idx_refs[pi][0].astype(BF16), True))
            lg_ref[step, :, lo:hi] = logits([page_rows(kp_refs[pi], g) for g in range(DSA_KV_HEADS)])

    @pl.when(step == ns - 1)
    def _():
        zpad = lambda z: jnp.concatenate(
            [z, jnp.zeros((PAGE_SIZE - nq, z.shape[1]), z.dtype)], axis=0)
        sm = sm_ref[...]
        n_idx = lax.broadcasted_iota(I32, (nq, PAGE_SIZE), 1)
        q_idx = lax.broadcasted_iota(I32, (nq, PAGE_SIZE), 0)
        sc_new = idx_scores(zpad(sm[:, SM_IK:SM_IK + IDX_DIM]).astype(BF16), False)
        keys_ref[ns, :, 0:PAGE_SIZE] = jnp.where(n_idx <= q_idx, _to_key(sc_new), INT_MIN)
        dkp = zpad(dk_ref[...])
        lg_ref[ns, :, 0:PAGE_SIZE] = logits([dkp[:, g * DSA_HD:(g + 1) * DSA_HD]
                                             for g in range(DSA_KV_HEADS)])

        lane = lax.broadcasted_iota(I32, (nq, SW), 1)

        def fold(fn, op):
            red, comb, init = _FOLD_OPS[op]
            acc = jnp.full((nq, PAGE_SIZE), init, I32)
            for s in range(ns):
                c = fn(keys_ref[s], s * SW + lane)
                for t in range(PP):
                    acc = comb(acc, c[:, t * PAGE_SIZE:(t + 1) * PAGE_SIZE])
            acc = comb(acc, fn(keys_ref[ns, :, 0:PAGE_SIZE], past_len + n_idx))
            return red(acc, axis=1, keepdims=True)

        n_valid = past_len + 1 + lax.broadcasted_iota(I32, (nq, 1), 0)
        tau, pc = _topk_search(fold, ktop, (nq, 1), nbits, n_valid)

        far = jnp.concatenate([jnp.full((nq, 1), rel_ref[BUCKET_FAR, h], F32)
                               for h in range(DSA_HEADS)], axis=0)
        near = [jnp.concatenate([_bias_from_buckets(bkt_ref[t], rel_ref, h)
                                 for h in range(DSA_HEADS)], axis=0) for t in range(2)]
        tile8 = lambda z: jnp.concatenate([z] * DSA_HEADS, axis=0)

        m = jnp.full((HR, 1), NEG_BIG, F32)
        for s in range(ns + 1):
            if s < ns:
                sel = _selected(keys_ref[s], s * SW + lane, tau, pc)
                bias = far
                lg = lg_ref[s] + tile8(jnp.where(sel, 0.0, NEG_BIG))
                if s == ns - 1:
                    lg = jnp.concatenate(
                        [lg[:, :SW - PAGE_SIZE] + far, lg[:, SW - PAGE_SIZE:] + near[0]], axis=1)
                else:
                    lg = lg + bias
                lg_ref[s] = lg
            else:
                sel = _selected(keys_ref[ns, :, 0:PAGE_SIZE], past_len + n_idx, tau, pc)
                lg = lg_ref[ns, :, 0:PAGE_SIZE] + tile8(jnp.where(sel, 0.0, NEG_BIG)) + near[1]
                lg_ref[ns, :, 0:PAGE_SIZE] = lg
            m = jnp.maximum(m, jnp.max(lg, axis=1, keepdims=True))

        l = jnp.zeros((HR, 1), F32)
        for s in range(ns):
            p = jnp.exp(lg_ref[s] - m)
            l = l + jnp.sum(p, axis=1, keepdims=True)
            p_ref[s] = p.astype(BF16)
        pn = jnp.exp(lg_ref[ns, :, 0:PAGE_SIZE] - m)
        l = l + jnp.sum(pn, axis=1, keepdims=True)
        linv_ref[...] = jnp.broadcast_to(1.0 / l, (HR, LANES))
        vn = zpad(dv_ref[...]).astype(BF16)
        pnb = pn.astype(BF16)
        for g in range(DSA_KV_HEADS):
            acc_ref[g * GR:(g + 1) * GR, :] = _dot(pnb[g * GR:(g + 1) * GR, :],
                                                   vn[:, g * DSA_HD:(g + 1) * DSA_HD])

    @pl.when(step >= ns)
    def _():
        s = step - ns
        for pi in range(PP):
            for g in range(DSA_KV_HEADS):
                acc_ref[g * GR:(g + 1) * GR, :] += _dot(
                    p_ref[s, g * GR:(g + 1) * GR, pi * PAGE_SIZE:(pi + 1) * PAGE_SIZE],
                    page_rows(vp_refs[pi], g).astype(BF16))

    @pl.when(step == 2 * ns - 1)
    def _():
        out = acc_ref[...] * linv_ref[...]
        for h in range(DSA_HEADS):
            o_ref[:, h * DSA_HD:(h + 1) * DSA_HD] = out[h * nq:(h + 1) * nq, :]


def _dsa_sample(page_table, rel_bias, dq, iq, sm, dk, dv, cache_idx, cache_k, cache_v, layer, n_pool):
    db, n_pages = page_table.shape
    nq = dq.shape[0] // db
    assert nq == SUBLANES and n_pages % PP == 0 and nq <= PAGE_SIZE
    ns = n_pages // PP
    past_len = n_pages * PAGE_SIZE
    ktop = min(TOPK_MAX, (past_len + nq) // 4)
    nbits = int(past_len + nq).bit_length()
    base = layer * n_pool
    q_idx = np.arange(nq)[:, None]
    lane = np.arange(PAGE_SIZE)[None, :]
    bkt = np.stack([_rel_bucket_np(PAGE_SIZE + q_idx - lane), _rel_bucket_np(q_idx - lane)])
    hr = DSA_HEADS * nq

    def page_spec(rows, width, phase, pi):
        if phase == 0:
            fn = lambda b, s, pt: (base + pt[b, jnp.minimum(s, ns - 1) * PP + pi], 0, 0)
        else:
            fn = lambda b, s, pt: (base + pt[b, jnp.maximum(s - ns, 0) * PP + pi], 0, 0)
        return pl.BlockSpec((1, rows, width), fn)

    tok = lambda wd: pl.BlockSpec((nq, wd), lambda b, s, pt: (b, 0))
    grid_spec = pltpu.PrefetchScalarGridSpec(
        num_scalar_prefetch=1,
        grid=(db, 2 * ns),
        in_specs=([pl.BlockSpec(memory_space=pltpu.SMEM),
                   pl.BlockSpec((2, nq, PAGE_SIZE), lambda b, s, pt: (0, 0, 0)),
                   tok(DSA_QW), tok(IDX_QW), tok(LANES), tok(DSA_KVW), tok(DSA_KVW)]
                  + [page_spec(IDX_DIM, PAGE_SIZE, 0, pi) for pi in range(PP)]
                  + [page_spec(PAGE_SIZE * DSA_KV_HEADS, DSA_HD, 0, pi) for pi in range(PP)]
                  + [page_spec(PAGE_SIZE * DSA_KV_HEADS, DSA_HD, 1, pi) for pi in range(PP)]),
        out_specs=tok(DSA_QW),
        scratch_shapes=[pltpu.VMEM((ns + 1, nq, PP * PAGE_SIZE), I32),
                        pltpu.VMEM((ns + 1, hr, PP * PAGE_SIZE), F32),
                        pltpu.VMEM((ns, hr, PP * PAGE_SIZE), BF16),
                        pltpu.VMEM((hr, DSA_HD), F32),
                        pltpu.VMEM((hr, LANES), F32),
                        pltpu.VMEM((hr, IDX_DIM), BF16),
                        pltpu.VMEM((hr, DSA_HD), BF16),
                        pltpu.VMEM((hr, LANES), F32)])
    return pl.pallas_call(
        functools.partial(_dsa_sample_kernel, nq=nq, ns=ns, ktop=ktop, nbits=nbits, past_len=past_len),
        grid_spec=grid_spec,
        out_shape=jax.ShapeDtypeStruct((db * nq, DSA_QW), F32),
        compiler_params=_params(("arbitrary", "arbitrary")),
        name="dsa_sample",
    )(page_table, rel_bias, jnp.asarray(bkt), dq, iq, sm, dk, dv,
      *([cache_idx] * PP), *([cache_k] * PP), *([cache_v] * PP))


def _merge_kernel(x_ref, oa_ref, ob_ref, ga_ref, gb_ref, wpa_ref, wpb_ref, wout_ref, gpost_ref, o_ref):
    pa = _dot(oa_ref[...].astype(BF16), wpa_ref[...])
    pb = _dot(ob_ref[...].astype(BF16), wpb_ref[...])
    m = _sigmoid(ga_ref[...]) * pa + _sigmoid(gb_ref[...]) * pb
    mo = _dot(m.astype(BF16), wout_ref[...])
    o_ref[...] = x_ref[...] + _rms(mo, gpost_ref[...])


def _merge(x, oa, ob, ga, gb, wpa, wpb, wout, gpost, tm):
    n = x.shape[0]
    assert n % tm == 0
    row = pl.BlockSpec((tm, D_MODEL), lambda i: (i, 0))
    wsp = pl.BlockSpec((D_MODEL, D_MODEL), lambda i: (0, 0))
    return pl.pallas_call(
        _merge_kernel,
        grid=(n // tm,),
        in_specs=[row, row, row, row, row, wsp, wsp, wsp, pl.BlockSpec((1, D_MODEL), lambda i: (0, 0))],
        out_specs=row,
        out_shape=jax.ShapeDtypeStruct((n, D_MODEL), F32),
        compiler_params=_params(("arbitrary",)),
        name="merge",
    )(x, oa, ob, ga, gb, wpa, wpb, wout, gpost)


FFN_TF = 1024
PREV_ROWS = 16


def _ffn_kernel(x_ref, xprev_ref, gpre_ref, gpost_ref, wg_ref, wu_ref, cw_ref, cb_ref, wd_ref, st_ref,
                o_ref, tail_ref, h2_ref, hp_ref, gext_ref, acc_ref,
                *, shift, nprev, tiles_per_seq, use_state, tail_off):
    i = pl.program_id(0)
    j = pl.program_id(1)
    tm = x_ref.shape[0]

    @pl.when(j == 0)
    def _():
        h2_ref[...] = _rms(x_ref[...], gpre_ref[...]).astype(BF16)
        hp_ref[...] = _rms(xprev_ref[...], gpre_ref[...]).astype(BF16)
        acc_ref[...] = jnp.zeros_like(acc_ref)

    h2 = h2_ref[...]
    g = _dot(h2, wg_ref[...])
    u = _dot(h2, wu_ref[...])
    if use_state:
        gprev = st_ref[...]
    else:
        gprev = _dot(hp_ref[...], wg_ref[...])
        gprev = jnp.where(i % tiles_per_seq == 0, 0.0, gprev)
    gext_ref[0:nprev, :] = gprev
    gext_ref[nprev:nprev + tm, :] = g
    cw = cw_ref[...]
    c = (cb_ref[...] + gext_ref[nprev - 2 * shift:nprev - 2 * shift + tm, :] * cw[0:1, :]
         + gext_ref[nprev - shift:nprev - shift + tm, :] * cw[1:2, :] + g * cw[2:3, :])
    gelu = 0.5 * c * (1.0 + jnp.tanh(math.sqrt(2.0 / math.pi) * (c + 0.044715 * (c * c * c))))
    acc_ref[...] += _dot((gelu * u).astype(BF16), wd_ref[...])
    tail_ref[...] = g[tail_off:tail_off + tail_ref.shape[0], :]

    @pl.when(j == pl.num_programs(1) - 1)
    def _():
        o_ref[...] = x_ref[...] + _rms(acc_ref[...], gpost_ref[...])


def _ffn(x, gpre, gpost, w_in_bf, conv_w, conv_b, w_down_bf, state, *, tm, shift, tiles_per_seq,
         use_state, tail_off, tail_rows):
    n = x.shape[0]
    assert n % tm == 0 and tm % PREV_ROWS == 0 and D_FF % FFN_TF == 0
    nprev = state.shape[0] if use_state else PREV_ROWS
    assert nprev >= 2 * shift and (not use_state or n == tm)
    nj = D_FF // FFN_TF
    pr = tm // PREV_ROWS
    return pl.pallas_call(
        functools.partial(_ffn_kernel, shift=shift, nprev=nprev, tiles_per_seq=tiles_per_seq,
                          use_state=use_state, tail_off=tail_off),
        grid=(n // tm, nj),
        in_specs=[pl.BlockSpec((tm, D_MODEL), lambda i, j: (i, 0)),
                  pl.BlockSpec((PREV_ROWS, D_MODEL), lambda i, j: (jnp.maximum(i * pr - 1, 0), 0)),
                  pl.BlockSpec((1, D_MODEL), lambda i, j: (0, 0)),
                  pl.BlockSpec((1, D_MODEL), lambda i, j: (0, 0)),
                  pl.BlockSpec((D_MODEL, FFN_TF), lambda i, j: (0, j)),
                  pl.BlockSpec((D_MODEL, FFN_TF), lambda i, j: (0, j + nj)),
                  pl.BlockSpec((CONV_W, FFN_TF), lambda i, j: (0, j)),
                  pl.BlockSpec((1, FFN_TF), lambda i, j: (0, j)),
                  pl.BlockSpec((FFN_TF, D_MODEL), lambda i, j: (j, 0)),
                  pl.BlockSpec((state.shape[0], FFN_TF), lambda i, j: (0, j))],
        out_specs=[pl.BlockSpec((tm, D_MODEL), lambda i, j: (i, 0)),
                   pl.BlockSpec((tail_rows, FFN_TF), lambda i, j: (i, j))],
        out_shape=[jax.ShapeDtypeStruct((n, D_MODEL), F32),
                   jax.ShapeDtypeStruct((n // tm * tail_rows, D_FF), F32)],
        scratch_shapes=[pltpu.VMEM((tm, D_MODEL), BF16),
                        pltpu.VMEM((PREV_ROWS, D_MODEL), BF16),
                        pltpu.VMEM((nprev + tm, FFN_TF), F32),
                        pltpu.VMEM((tm, D_MODEL), F32)],
        compiler_params=_params(("arbitrary", "arbitrary")),
        name="ffn",
    )(x, x, gpre, gpost, w_in_bf, w_in_bf, conv_w, conv_b, w_down_bf, state)


def _round_up(a, m):
    return -(-a // m) * m


def kernel(x_prompt, x_sample, cache_k, cache_v, cache_idx_k, state_gla, state_conv, page_table, meta_tokens, rel_bias, norm_mix_pre, norm_mix_post, norm_ffn_pre, norm_ffn_post, w_in, w_a2, b_a, gla_norm, w_pa, w_pb, w_out, w_ffn_in, conv_w, conv_b, w_ffn_down):
    bsz, seq = x_prompt.shape[:2]
    db, nq = x_sample.shape[:2]
    depth = w_in.shape[0]
    n_pool = cache_k.shape[1]
    t_real = seq + N_META
    tp = _round_up(t_real, QB)
    nb = tp // QB
    row2 = lambda a: a.reshape(1, -1)
    seg_names = [n for n, _ in SEGS] + ["bcum"]

    w_in_r = [_relayout_w_in(w_in[l]) for l in range(depth)]
    w_pa_b, w_pb_b, w_out_b = w_pa.astype(BF16), w_pb.astype(BF16), w_out.astype(BF16)
    w_ffn_in_b, w_ffn_down_b = w_ffn_in.astype(BF16), w_ffn_down.astype(BF16)

    ck = cache_k.reshape(depth * n_pool, PAGE_SIZE * DSA_KV_HEADS, DSA_HD)
    cv = cache_v.reshape(depth * n_pool, PAGE_SIZE * DSA_KV_HEADS, DSA_HD)
    ci = cache_idx_k.transpose(0, 1, 3, 2).reshape(depth * n_pool, IDX_DIM, PAGE_SIZE)

    xp = jnp.concatenate([jnp.broadcast_to(meta_tokens[None].astype(x_prompt.dtype), (bsz, N_META, D_MODEL)),
                          x_prompt, jnp.zeros((bsz, tp - t_real, D_MODEL), x_prompt.dtype)], axis=1)
    xp = xp.reshape(bsz * tp, D_MODEL)
    ffn_tm = tp // 4 if (tp // 4) % PREV_ROWS == 0 else tp
    tiles_per_seq = tp // ffn_tm
    tail_pos = (t_real - (CONV_W - 1)) % ffn_tm
    tail_off = tail_pos // SUBLANES * SUBLANES
    assert tail_pos - tail_off + (CONV_W - 1) <= SUBLANES
    pk, pv, pik, pgla, pconv = [], [], [], [], []
    zero_state = jnp.zeros((bsz, GLA_HEADS, GLA_DK, GLA_DV), F32)
    zero_conv = jnp.zeros((PREV_ROWS, D_FF), F32)
    for l in range(depth):
        z = dict(zip(seg_names, _proj(xp, row2(norm_mix_pre[l]), w_in_r[l], w_a2[l], row2(b_a[l]),
                                      2 * CHUNK, CHUNK, tp, t_real)))
        r3 = lambda a: a.reshape(bsz, tp, a.shape[-1])
        o_a, s_fin = _gla(r3(z["gq"]), r3(z["gk"]), r3(z["gv"]), r3(z["gr"]), r3(z["bcum"]),
                          row2(gla_norm[l]), zero_state, 0, t_real)
        dk3, dv3, sm3 = r3(z["dk"]), r3(z["dv"]), r3(z["sm"])
        o_b = _dsa_prompt(rel_bias, r3(z["dq"]), r3(z["iq"]), sm3, dk3, dv3, t_real)
        xm = _merge(xp, o_a.reshape(bsz * tp, GLA_VW), o_b.reshape(bsz * tp, DSA_QW), z["ga"], z["gb"],
                    w_pa_b[l], w_pb_b[l], w_out_b[l], row2(norm_mix_post[l]), 512 if (bsz * tp) % 512 == 0 else CHUNK)
        xp, tail = _ffn(xm, row2(norm_ffn_pre[l]), row2(norm_ffn_post[l]), w_ffn_in_b[l], conv_w[l],
                        row2(conv_b[l]), w_ffn_down_b[l], zero_conv, tm=ffn_tm, shift=1,
                        tiles_per_seq=tiles_per_seq, use_state=False, tail_off=tail_off, tail_rows=SUBLANES)
        pk.append(dk3[:, :t_real].reshape(bsz, t_real, DSA_KV_HEADS, DSA_HD))
        pv.append(dv3[:, :t_real].reshape(bsz, t_real, DSA_KV_HEADS, DSA_HD))
        pik.append(sm3[:, :t_real, SM_IK:SM_IK + IDX_DIM])
        pgla.append(s_fin)
        tail = tail.reshape(bsz, tiles_per_seq, SUBLANES, D_FF)[:, (t_real - 1) // ffn_tm]
        pconv.append(tail[:, tail_pos - tail_off:tail_pos - tail_off + CONV_W - 1])
    y_prompt = xp.reshape(bsz, tp, D_MODEL)[:, N_META:t_real]

    ns_rows = db * nq
    xs = x_sample.reshape(ns_rows, D_MODEL)
    sk, sv, sik, sgla, sconv = [], [], [], [], []
    state_all = state_gla.reshape(depth * db, GLA_HEADS, GLA_DK, GLA_DV)
    for l in range(depth):
        z = dict(zip(seg_names, _proj(xs, row2(norm_mix_pre[l]), w_in_r[l], w_a2[l], row2(b_a[l]),
                                      ns_rows, nq, nq, nq)))
        r3 = lambda a: a.reshape(db, nq, a.shape[-1])
        o_a, s_fin = _gla(r3(z["gq"]), r3(z["gk"]), r3(z["gv"]), r3(z["gr"]), r3(z["bcum"]),
                          row2(gla_norm[l]), state_all, l * db, nq)
        o_b = _dsa_sample(page_table, rel_bias, z["dq"], z["iq"], z["sm"], z["dk"], z["dv"],
                          ci, ck, cv, l, n_pool)
        xm = _merge(xs, o_a.reshape(ns_rows, GLA_VW), o_b, z["ga"], z["gb"],
                    w_pa_b[l], w_pb_b[l], w_out_b[l], row2(norm_mix_post[l]), ns_rows)
        xm_t = xm.reshape(db, nq, D_MODEL).transpose(1, 0, 2).reshape(ns_rows, D_MODEL)
        st = state_conv[l].transpose(1, 0, 2).reshape((CONV_W - 1) * db, D_FF)
        xo_t, tail = _ffn(xm_t, row2(norm_ffn_pre[l]), row2(norm_ffn_post[l]), w_ffn_in_b[l], conv_w[l],
                          row2(conv_b[l]), w_ffn_down_b[l], st, tm=ns_rows, shift=db, tiles_per_seq=1,
                          use_state=True, tail_off=(nq - (CONV_W - 1)) * db, tail_rows=(CONV_W - 1) * db)
        xs = xo_t.reshape(nq, db, D_MODEL).transpose(1, 0, 2).reshape(ns_rows, D_MODEL)
        sk.append(z["dk"].reshape(db, nq, DSA_KV_HEADS, DSA_HD))
        sv.append(z["dv"].reshape(db, nq, DSA_KV_HEADS, DSA_HD))
        sik.append(z["sm"][:, SM_IK:SM_IK + IDX_DIM].reshape(db, nq, IDX_DIM))
        sgla.append(s_fin)
        sconv.append(tail.reshape(CONV_W - 1, db, D_FF).transpose(1, 0, 2))
    y_sample = xs.reshape(db, nq, D_MODEL)

    return (y_prompt, y_sample,
            jnp.stack(pk), jnp.stack(pv), jnp.stack(pik), jnp.stack(pgla), jnp.stack(pconv),
            jnp.stack(sk), jnp.stack(sv), jnp.stack(sik), jnp.stack(sgla), jnp.stack(sconv))
```
